```python
import functools
import jax, jax.numpy as jnp
from jax import lax
import numpy as np

D_MODEL = 2048
BATCH = 2
SEQ = 4096
DEPTH = 4
DEC_BATCH = 8
DEC_SEQ = 8
PAST_LEN = 16384
PAGE_SIZE = 128

N_EVEN = (DEPTH + 1) // 2
N_ODD = DEPTH // 2
D_ATTN = D_MODEL // 2
HEAD_DIM = 128
N_HEADS = D_ATTN // HEAD_DIM
PATTERNS = ((128, 1), (512, 4), (2048, 16))
WINDOW_MAX = max(w for w, _ in PATTERNS)
Q_BLOCK = 128
ROPE_THETA = 10000.0
D_POOL = D_MODEL // 2
POOL_WINDOWS = (2, 4, 8, 16)
N_POOL_GROUPS = len(POOL_WINDOWS)
POOL_GROUP = D_POOL // N_POOL_GROUPS
POOL_MAX = max(POOL_WINDOWS)
D_CONV = D_MODEL
CONV_WIDTH = 3
EVEN_SPLITS = (D_ATTN, 2 * D_ATTN, 3 * D_ATTN, 4 * D_ATTN, 4 * D_ATTN + D_POOL)
D_IN_EVEN = 4 * D_ATTN + 2 * D_POOL
D_IN_ODD = 4 * D_CONV
RMS_EPS = 1e-6
NEG_INF = -1e30

kernel_name = 'hybrid_dilated_pool_shortconv_decoder_step'


def rms_norm(x, g):
    xf = x.astype(jnp.float32)
    y = xf * lax.rsqrt(jnp.mean(xf * xf, axis=-1, keepdims=True) + RMS_EPS)
    return (y * g.astype(jnp.float32)).astype(x.dtype)


def ada_modulation(c, w, b):
    m = jax.nn.silu(c) @ w + b
    shift, scale, gate = jnp.split(m[:, None, :], 3, axis=-1)
    return shift, scale, gate


def rotary(x, pos):
    half = HEAD_DIM // 2
    inv = jnp.power(ROPE_THETA, -jnp.arange(half, dtype=jnp.float32) / half)
    ang = pos.astype(jnp.float32)[:, None] * inv[None, :]
    cos = jnp.cos(ang)[None, :, None, :]
    sin = jnp.sin(ang)[None, :, None, :]
    xf = x.astype(jnp.float32)
    x1, x2 = xf[..., :half], xf[..., half:]
    return jnp.concatenate([x1 * cos - x2 * sin, x2 * cos + x1 * sin], axis=-1).astype(x.dtype)


def dilated_attention(q, k_ext, v_ext, n_prefix):
    b, t = q.shape[0], q.shape[1]
    qb = Q_BLOCK if t % Q_BLOCK == 0 else t
    scale = HEAD_DIM ** -0.5
    kf = k_ext.astype(jnp.float32)
    vf = v_ext.astype(jnp.float32)

    def block(i):
        qs = lax.dynamic_slice_in_dim(q, i * qb, qb, axis=1).astype(jnp.float32) * scale
        e = n_prefix + i * qb + jnp.arange(qb, dtype=jnp.int32)
        ms, dens, nums = [], [], []
        for w, d in PATTERNS:
            idx = e[:, None] - d * jnp.arange(w // d + 1, dtype=jnp.int32)[None, :]
            valid = idx >= 0
            idx = jnp.maximum(idx, 0)
            kg = kf[:, idx]
            vg = vf[:, idx]
            s = jnp.einsum('bqhd,bqjhd->bhqj', qs, kg)
            s = jnp.where(valid[None, None], s, NEG_INF)
            m = jnp.max(s, axis=-1, keepdims=True)
            p = jnp.exp(s - m)
            ms.append(m)
            dens.append(jnp.sum(p, axis=-1, keepdims=True))
            nums.append(jnp.einsum('bhqj,bqjhd->bhqd', p, vg))
        m_all = functools.reduce(jnp.maximum, ms)
        a = [jnp.exp(m - m_all) for m in ms]
        num = a[0] * nums[0]
        den = a[0] * dens[0]
        for ai, ni, di in zip(a[1:], nums[1:], dens[1:]):
            num = num + ai * ni
            den = den + ai * di
        return jnp.transpose(num / den, (0, 2, 1, 3))

    out = lax.map(block, jnp.arange(t // qb, dtype=jnp.int32))
    out = jnp.transpose(out, (1, 0, 2, 3, 4)).reshape(b, t, N_HEADS * HEAD_DIM)
    return out.astype(q.dtype)


def pool_mixer(u, prefix, pos, pool_w, pool_scale):
    b, t = u.shape[0], u.shape[1]
    npre = POOL_MAX - 1
    ext = jnp.concatenate([prefix, u], axis=1).astype(jnp.float32)
    cs = jnp.concatenate([jnp.zeros((b, 1, D_POOL), jnp.float32), jnp.cumsum(ext, axis=1)], axis=1)
    uf = u.astype(jnp.float32)
    outs = []
    for gi, w in enumerate(POOL_WINDOWS):
        lo, hi = gi * POOL_GROUP, (gi + 1) * POOL_GROUP
        win_sum = cs[:, npre + 1:npre + 1 + t, lo:hi] - cs[:, npre + 1 - w:npre + 1 - w + t, lo:hi]
        cnt = jnp.minimum(w, pos + 1).astype(jnp.float32)[None, :, None]
        outs.append(win_sum / cnt - uf[..., lo:hi])
    pooled = jnp.stack(outs, axis=2)
    mixed = jnp.einsum('btgc,gcd->btgd', pooled, pool_w.astype(jnp.float32)).reshape(b, t, D_POOL)
    return (mixed * pool_scale.astype(jnp.float32)).astype(u.dtype)


def short_conv(z, prefix, conv_w):
    t = z.shape[1]
    ext = jnp.concatenate([prefix, z], axis=1)
    y = ext[:, 0:t] * conv_w[0]
    for kk in range(1, CONV_WIDTH):
        y = y + ext[:, kk:kk + t] * conv_w[kk]
    return y, ext[:, -(CONV_WIDTH - 1):]


def run_trunk(x, c, pos, win_k, win_v, pool_st, conv_st, ada_w, ada_b, norm_g, w_in_even,
              w_out_even, pool_w, pool_scale, w_in_odd, conv_w, w_out_odd, final_g):
    b, t = x.shape[0], x.shape[1]
    fresh = win_k is None
    new_k, new_v, new_pool, new_conv = [], [], [], []
    for layer in range(DEPTH):
        shift, scale, gate = ada_modulation(c, ada_w[layer], ada_b[layer])
        h = rms_norm(x, norm_g[layer]) * (1 + scale) + shift
        li = layer // 2
        if layer % 2 == 0:
            q, k, v, g_a, u, g_p = jnp.split(h @ w_in_even[li], EVEN_SPLITS, axis=-1)
            q = rotary(q.reshape(b, t, N_HEADS, HEAD_DIM), pos)
            k = rotary(k.reshape(b, t, N_HEADS, HEAD_DIM), pos)
            v = v.reshape(b, t, N_HEADS, HEAD_DIM)
            if fresh:
                k_ext, v_ext, n_prefix = k, v, 0
                keep = min(WINDOW_MAX, t)
                new_k.append(k[:, t - keep:])
                new_v.append(v[:, t - keep:])
                pool_prefix = jnp.zeros((b, POOL_MAX - 1, D_POOL), u.dtype)
            else:
                k_ext = jnp.concatenate([win_k[li], k], axis=1)
                v_ext = jnp.concatenate([win_v[li], v], axis=1)
                n_prefix = win_k.shape[2]
                new_k.append(k)
                new_v.append(v)
                pool_prefix = pool_st[li]
            attn = dilated_attention(q, k_ext, v_ext, n_prefix)
            pooled = pool_mixer(u, pool_prefix, pos, pool_w[li], pool_scale[li])
            new_pool.append(jnp.concatenate([pool_prefix, u], axis=1)[:, -(POOL_MAX - 1):])
            merged = jnp.concatenate([attn * jax.nn.silu(g_a), pooled * jax.nn.silu(g_p)], axis=-1)
            mix = merged @ w_out_even[li]
        else:
            bg, cg, hh, g = jnp.split(h @ w_in_odd[li], 4, axis=-1)
            z = cg * hh
            prefix = jnp.zeros((b, CONV_WIDTH - 1, D_CONV), z.dtype) if fresh else conv_st[li]
            yc, tail = short_conv(z, prefix, conv_w[li])
            new_conv.append(tail)
            mix = (bg * yc * jax.nn.silu(g)) @ w_out_odd[li]
        x = x + gate * mix
    return (rms_norm(x, final_g), jnp.stack(new_k), jnp.stack(new_v),
            jnp.stack(new_pool), jnp.stack(new_conv))


def setup_inputs(seed: int = 0) -> dict:
    key = jax.random.key(seed)
    ks = jax.random.split(key, 20)
    f32 = jnp.float32
    wbuf = min(WINDOW_MAX, PAST_LEN)
    nrm = lambda k, s: jax.random.normal(k, s, f32)
    return {
        'x_prompt': nrm(ks[0], (BATCH, SEQ, D_MODEL)),
        'x_sample': nrm(ks[1], (DEC_BATCH, DEC_SEQ, D_MODEL)),
        'c_prompt': nrm(ks[2], (BATCH, D_MODEL)),
        'c_sample': nrm(ks[3], (DEC_BATCH, D_MODEL)),
        'state_win_k': nrm(ks[4], (N_EVEN, DEC_BATCH, wbuf, N_HEADS, HEAD_DIM)),
        'state_win_v': nrm(ks[5], (N_EVEN, DEC_BATCH, wbuf, N_HEADS, HEAD_DIM)),
        'state_pool': nrm(ks[6], (N_EVEN, DEC_BATCH, POOL_MAX - 1, D_POOL)),
        'state_conv': nrm(ks[7], (N_ODD, DEC_BATCH, CONV_WIDTH - 1, D_CONV)),
        'ada_w': nrm(ks[8], (DEPTH, D_MODEL, 3 * D_MODEL)) * (0.5 * D_MODEL ** -0.5),
        'ada_b': nrm(ks[9], (DEPTH, 3 * D_MODEL)) * 0.02,
        'norm_g': 1.0 + 0.02 * nrm(ks[10], (DEPTH, D_MODEL)),
        'w_in_even': nrm(ks[11], (N_EVEN, D_MODEL, D_IN_EVEN)) * D_MODEL ** -0.5,
        'w_out_even': nrm(ks[12], (N_EVEN, D_ATTN + D_POOL, D_MODEL)) * (D_ATTN + D_POOL) ** -0.5,
        'pool_w': nrm(ks[13], (N_EVEN, N_POOL_GROUPS, POOL_GROUP, POOL_GROUP)) * POOL_GROUP ** -0.5,
        'pool_scale': 1.0 + 0.1 * nrm(ks[14], (N_EVEN, D_POOL)),
        'w_in_odd': nrm(ks[15], (N_ODD, D_MODEL, D_IN_ODD)) * D_MODEL ** -0.5,
        'conv_w': nrm(ks[16], (N_ODD, CONV_WIDTH, D_CONV)) * CONV_WIDTH ** -0.5,
        'w_out_odd': nrm(ks[17], (N_ODD, D_CONV, D_MODEL)) * D_CONV ** -0.5,
        'final_g': 1.0 + 0.02 * nrm(ks[18], (D_MODEL,)),
    }


def reference(x_prompt, x_sample, c_prompt, c_sample, state_win_k, state_win_v, state_pool,
              state_conv, ada_w, ada_b, norm_g, w_in_even, w_out_even, pool_w, pool_scale,
              w_in_odd, conv_w, w_out_odd, final_g):
    params = (ada_w, ada_b, norm_g, w_in_even, w_out_even, pool_w, pool_scale,
              w_in_odd, conv_w, w_out_odd, final_g)
    pos_p = jnp.arange(x_prompt.shape[1], dtype=jnp.int32)
    pos_s = PAST_LEN + jnp.arange(x_sample.shape[1], dtype=jnp.int32)
    y_prompt, kp, vp, poolp, convp = run_trunk(x_prompt, c_prompt, pos_p, None, None, None, None, *params)
    y_sample, ksm, vsm, pools, convs = run_trunk(x_sample, c_sample, pos_s, state_win_k, state_win_v,
                                                 state_pool, state_conv, *params)
    return (y_prompt, y_sample, kp, vp, poolp, convp, ksm, vsm, pools, convs)
```

```python
import functools

import numpy as np
import jax
import jax.numpy as jnp
from jax import lax
from jax.experimental import pallas as pl
from jax.experimental.pallas import tpu as pltpu

F32 = jnp.float32
BF16 = jnp.bfloat16

D_MODEL = 2048
DEPTH = 4
PAST_LEN = 16384
D_ATTN = D_MODEL // 2
HEAD_DIM = 128
N_HEADS = D_ATTN // HEAD_DIM
PATTERNS = ((128, 1), (512, 4), (2048, 16))
KEYS_PER_PATTERN = 128
WINDOW_MAX = max(w for w, _ in PATTERNS)
ROPE_THETA = 10000.0
D_POOL = D_MODEL // 2
POOL_WINDOWS = (2, 4, 8, 16)
POOL_GROUP = D_POOL // len(POOL_WINDOWS)
POOL_MAX = max(POOL_WINDOWS)
D_CONV = D_MODEL
CONV_WIDTH = 3
D_IN_EVEN = 4 * D_ATTN + 2 * D_POOL
RMS_EPS = 1e-6
NEG_INF = -1e30

LANES = 128
SUBLANES = 8
VMEM_LIMIT = 48 * 1024 * 1024


def _params(*semantics):
    return pltpu.CompilerParams(dimension_semantics=semantics,
                                vmem_limit_bytes=VMEM_LIMIT)


def _silu(v):
    return v / (1.0 + jnp.exp(-v))


def _dot(a, b):
    return jnp.dot(a, b, preferred_element_type=F32)


def _dot_nt(a, b):
    return lax.dot_general(a, b, (((1,), (1,)), ((), ())), preferred_element_type=F32)


def _rope_kernel(inv_ref, cos_ref, sin_ref, *, pos0):
    t = cos_ref.shape[0]
    pos = (lax.broadcasted_iota(jnp.int32, (t, HEAD_DIM), 0) + pos0).astype(F32)
    lane = lax.broadcasted_iota(jnp.int32, (t, HEAD_DIM), 1)
    ang = pos * inv_ref[...]
    s = jnp.sin(ang)
    cos_ref[...] = jnp.cos(ang)
    sin_ref[...] = jnp.where(lane < HEAD_DIM // 2, -s, s)


def _rope_tables(t, pos0):
    half = HEAD_DIM // 2
    inv = np.power(ROPE_THETA, -np.arange(half, dtype=np.float64) / half).astype(np.float32)
    inv = jnp.asarray(np.concatenate([inv, inv])[None, :])
    return pl.pallas_call(
        functools.partial(_rope_kernel, pos0=pos0),
        out_shape=(jax.ShapeDtypeStruct((t, HEAD_DIM), F32),
                   jax.ShapeDtypeStruct((t, HEAD_DIM), F32)),
        name="rope_tables",
    )(inv)


def _ada_kernel(c_ref, w_ref, b_ref, o_ref):
    a = _silu(c_ref[...]).astype(BF16)
    o_ref[...] = _dot(a, w_ref[...].astype(BF16)) + b_ref[...]


def _ada_modulation(c_all, ada_w, ada_b):
    rows = c_all.shape[0]
    n_out = ada_w.shape[2]
    tn = 1024
    return pl.pallas_call(
        _ada_kernel,
        grid=(DEPTH, n_out // tn),
        in_specs=[
            pl.BlockSpec((rows, D_MODEL), lambda l, n: (0, 0)),
            pl.BlockSpec((None, D_MODEL, tn), lambda l, n: (l, 0, n)),
            pl.BlockSpec((None, 1, tn), lambda l, n: (l, 0, n)),
        ],
        out_specs=pl.BlockSpec((None, rows, tn), lambda l, n: (l, 0, n)),
        out_shape=jax.ShapeDtypeStruct((DEPTH, rows, n_out), F32),
        compiler_params=_params("arbitrary", "arbitrary"),
        name="ada_modulation",
    )(c_all, ada_w, ada_b.reshape(DEPTH, 1, n_out))


def _modulated_norm(x_ref, shift_ref, scale_ref, g_ref):
    x = x_ref[...]
    ms = jnp.mean(x * x, axis=-1, keepdims=True)
    y = x * lax.rsqrt(ms + RMS_EPS) * g_ref[...]
    h = y * (1.0 + scale_ref[...]) + shift_ref[...]
    bb, tm, d = h.shape
    return h.reshape(bb * tm, d).astype(BF16)


def _in_even_kernel(x_ref, shift_ref, scale_ref, g_ref, w_ref, cos_ref, sin_ref,
                    o_ref, h_ref, *, n_rope_tiles):
    n = pl.program_id(2)
    tn = w_ref.shape[1]

    @pl.when(n == 0)
    def _():
        h_ref[...] = _modulated_norm(x_ref, shift_ref, scale_ref, g_ref)

    acc = _dot(h_ref[...], w_ref[...])

    @pl.when(n < n_rope_tiles)
    def _():
        cos = cos_ref[...]
        sin = sin_ref[...]
        for j in range(tn // HEAD_DIM):
            a = acc[:, j * HEAD_DIM:(j + 1) * HEAD_DIM]
            o_ref[:, j * HEAD_DIM:(j + 1) * HEAD_DIM] = (
                a * cos + pltpu.roll(a, HEAD_DIM // 2, axis=1) * sin)

    @pl.when(n >= n_rope_tiles)
    def _():
        o_ref[...] = acc


def _in_even(x, mod, g, w, cos_rows, sin_rows, bb, tm):
    bk, t, d = x.shape
    n_out = w.shape[1]
    tn = 512
    rows = bb * tm
    grid = (bk // bb, t // tm, n_out // tn)
    return pl.pallas_call(
        functools.partial(_in_even_kernel, n_rope_tiles=2 * D_ATTN // tn),
        grid=grid,
        in_specs=[
            pl.BlockSpec((bb, tm, d), lambda b, i, n: (b, i, 0)),
            pl.BlockSpec((bb, 1, d), lambda b, i, n: (b, 0, 0)),
            pl.BlockSpec((bb, 1, d), lambda b, i, n: (b, 0, 1)),
            pl.BlockSpec((1, d), lambda b, i, n: (0, 0)),
            pl.BlockSpec((d, tn), lambda b, i, n: (0, n)),
            pl.BlockSpec((rows, HEAD_DIM), lambda b, i, n: (i, 0)),
            pl.BlockSpec((rows, HEAD_DIM), lambda b, i, n: (i, 0)),
        ],
        out_specs=pl.BlockSpec((None, rows, tn), lambda b, i, n: (b, i, n)),
        out_shape=jax.ShapeDtypeStruct((bk // bb, (t // tm) * rows, n_out), F32),
        scratch_shapes=[pltpu.VMEM((rows, d), BF16)],
        compiler_params=_params("arbitrary", "arbitrary", "arbitrary"),
        name="in_proj_even",
    )(x, mod, mod, g, w, cos_rows, sin_rows)


def _in_odd_kernel(x_ref, shift_ref, scale_ref, g_ref, wb_ref, wc_ref, wh_ref, wg_ref,
                   cw_ref, pre_ref, o_ref, tail_ref, h_ref, zbuf_ref):
    i = pl.program_id(1)
    n = pl.program_id(2)
    bb, tm, _ = x_ref.shape
    tn = wb_ref.shape[1]
    halo = SUBLANES
    tail_cols = pl.ds(pl.multiple_of(n * tn, tn), tn)

    @pl.when(n == 0)
    def _():
        h_ref[...] = _modulated_norm(x_ref, shift_ref, scale_ref, g_ref)

    h = h_ref[...]
    z = (_dot(h, wc_ref[...]) * _dot(h, wh_ref[...])).reshape(bb, tm, tn)

    @pl.when(i == 0)
    def _():
        zbuf_ref[:, 0:halo, :] = pre_ref[...]

    @pl.when(i > 0)
    def _():
        zbuf_ref[:, 0:halo, :] = tail_ref[:, :, tail_cols]

    zbuf_ref[:, halo:, :] = z
    cw = cw_ref[...]
    yc = zbuf_ref[:, halo - 2:halo - 2 + tm, :] * cw[0:1]
    yc = yc + zbuf_ref[:, halo - 1:halo - 1 + tm, :] * cw[1:2]
    yc = yc + z * cw[2:3]
    tail_ref[:, :, tail_cols] = zbuf_ref[:, tm:tm + halo, :]

    bg = _dot(h, wb_ref[...]).reshape(bb, tm, tn)
    gate = _dot(h, wg_ref[...]).reshape(bb, tm, tn)
    y = bg * yc * _silu(gate)
    o_ref[...] = y.reshape(bb * tm, tn).astype(BF16)


def _in_odd(x, mod, g, w, conv_w, prefix, bb, tm):
    bk, t, d = x.shape
    tn = 256
    nt = D_CONV // tn
    rows = bb * tm
    grid = (bk // bb, t // tm, nt)
    w_spec = lambda k: pl.BlockSpec((d, tn), lambda b, i, n, k=k: (0, k * nt + n))
    return pl.pallas_call(
        _in_odd_kernel,
        grid=grid,
        in_specs=[
            pl.BlockSpec((bb, tm, d), lambda b, i, n: (b, i, 0)),
            pl.BlockSpec((bb, 1, d), lambda b, i, n: (b, 0, 0)),
            pl.BlockSpec((bb, 1, d), lambda b, i, n: (b, 0, 1)),
            pl.BlockSpec((1, d), lambda b, i, n: (0, 0)),
            w_spec(0), w_spec(1), w_spec(2), w_spec(3),
            pl.BlockSpec((CONV_WIDTH, tn), lambda b, i, n: (0, n)),
            pl.BlockSpec((bb, SUBLANES, tn), lambda b, i, n: (b, 0, n)),
        ],
        out_specs=(
            pl.BlockSpec((None, rows, tn), lambda b, i, n: (b, i, n)),
            pl.BlockSpec((bb, SUBLANES, D_CONV), lambda b, i, n: (b, 0, 0)),
        ),
        out_shape=(
            jax.ShapeDtypeStruct((bk // bb, (t // tm) * rows, D_CONV), BF16),
            jax.ShapeDtypeStruct((bk, SUBLANES, D_CONV), F32),
        ),
        scratch_shapes=[
            pltpu.VMEM((rows, d), BF16),
            pltpu.VMEM((bb, SUBLANES + tm, tn), F32),
        ],
        compiler_params=_params("arbitrary", "arbitrary", "arbitrary"),
        name="in_proj_odd_conv",
    )(x, mod, mod, g, w, w, w, w, conv_w, prefix)


def _out_proj_kernel(a1_ref, a2_ref, w1_ref, w2_ref, x_ref, gate_ref, o_ref):
    bb, tm, tn = x_ref.shape
    acc = _dot(a1_ref[...].astype(BF16), w1_ref[...])
    acc = acc + _dot(a2_ref[...].astype(BF16), w2_ref[...])
    o_ref[...] = x_ref[...] + gate_ref[...] * acc.reshape(bb, tm, tn)


def _out_proj(a1, a1_blk, a2, a2_blk, w, x, mod, bb, tm):
    bk, t, d = x.shape
    kh = w.shape[0] // 2
    tn = 512
    rows = bb * tm
    gate_blk0 = 2 * d // tn
    return pl.pallas_call(
        _out_proj_kernel,
        grid=(bk // bb, t // tm, d // tn),
        in_specs=[
            pl.BlockSpec((None, rows, kh), lambda b, i, n: (b, i, a1_blk)),
            pl.BlockSpec((None, rows, kh), lambda b, i, n: (b, i, a2_blk)),
            pl.BlockSpec((kh, tn), lambda b, i, n: (0, n)),
            pl.BlockSpec((kh, tn), lambda b, i, n: (1, n)),
            pl.BlockSpec((bb, tm, tn), lambda b, i, n: (b, i, n)),
            pl.BlockSpec((bb, 1, tn), lambda b, i, n: (b, 0, gate_blk0 + n)),
        ],
        out_specs=pl.BlockSpec((bb, tm, tn), lambda b, i, n: (b, i, n)),
        out_shape=jax.ShapeDtypeStruct((bk, t, d), F32),
        compiler_params=_params("arbitrary", "arbitrary", "arbitrary"),
        name="out_proj_residual",
    )(a1, a2, w, w, x, mod)


def _attn_prompt_kernel(q_ref, k_ref, v_ref, ga_ref, o_ref, acc_ref, m_ref, l_ref):
    t = q_ref.shape[0]
    blk_q = KEYS_PER_PATTERN
    scale = HEAD_DIM ** -0.5
    row = lax.broadcasted_iota(jnp.int32, (blk_q, blk_q), 0)
    col = lax.broadcasted_iota(jnp.int32, (blk_q, blk_q), 1)
    cur_ok = col <= row
    prev_ok = col >= row

    for pi, (_, d) in enumerate(PATTERNS):
        span = blk_q * d
        nblk = t // span

        def rows(ref, start, d=d):
            if d == 1:
                return ref[pl.ds(pl.multiple_of(start, blk_q), blk_q), :]
            return ref[pl.ds(start, blk_q, stride=d), :]

        def put(ref, start, val, d=d):
            if d == 1:
                ref[pl.ds(pl.multiple_of(start, blk_q), blk_q), :] = val
            else:
                ref[pl.ds(start, blk_q, stride=d), :] = val

        def body(it, carry, pi=pi, span=span, nblk=nblk, rows=rows, put=put):
            r = it // nblk
            blk = it % nblk
            base = r + blk * span
            has_prev = blk > 0
            pbase = jnp.where(has_prev, base - span, base)
            q = (rows(q_ref, base) * scale).astype(BF16)
            kc = rows(k_ref, base).astype(BF16)
            kp = rows(k_ref, pbase).astype(BF16)
            vc = rows(v_ref, base).astype(BF16)
            vp = rows(v_ref, pbase).astype(BF16)
            sc = jnp.where(cur_ok, _dot_nt(q, kc), NEG_INF)
            sp = jnp.where(jnp.logical_and(prev_ok, has_prev), _dot_nt(q, kp), NEG_INF)
            m = jnp.maximum(jnp.max(sc, axis=-1, keepdims=True),
                            jnp.max(sp, axis=-1, keepdims=True))
            pc = jnp.exp(sc - m)
            pp = jnp.exp(sp - m)
            l = jnp.sum(pc, axis=-1, keepdims=True) + jnp.sum(pp, axis=-1, keepdims=True)
            num = _dot(pc.astype(BF16), vc) + _dot(pp.astype(BF16), vp)
            m = jnp.broadcast_to(m, (blk_q, HEAD_DIM))
            l = jnp.broadcast_to(l, (blk_q, HEAD_DIM))
            if pi > 0:
                m_old = rows(m_ref, base)
                m_new = jnp.maximum(m_old, m)
                a_old = jnp.exp(m_old - m_new)
                a_new = jnp.exp(m - m_new)
                num = a_old * rows(acc_ref, base) + a_new * num
                l = a_old * rows(l_ref, base) + a_new * l
                m = m_new
            put(acc_ref, base, num)
            put(m_ref, base, m)
            put(l_ref, base, l)
            return carry

        lax.fori_loop(0, t // blk_q, body, 0)

    o_ref[...] = (acc_ref[...] / l_ref[...] * _silu(ga_ref[...])).astype(BF16)


def _attn_prompt(proj):
    b, t, _ = proj.shape
    for w, d in PATTERNS:
        assert w // d == KEYS_PER_PATTERN and t % (KEYS_PER_PATTERN * d) == 0
    col = lambda off: pl.BlockSpec((None, t, HEAD_DIM), lambda bi, h, off=off: (bi, 0, off + h))
    return pl.pallas_call(
        _attn_prompt_kernel,
        grid=(b, N_HEADS),
        in_specs=[col(0), col(N_HEADS), col(2 * N_HEADS), col(3 * N_HEADS)],
        out_specs=pl.BlockSpec((None, t, HEAD_DIM), lambda bi, h: (bi, 0, h)),
        out_shape=jax.ShapeDtypeStruct((b, t, D_ATTN), BF16),
        scratch_shapes=[pltpu.VMEM((t, HEAD_DIM), F32)] * 3,
        compiler_params=_params("arbitrary", "arbitrary"),
        name="attn_prompt",
    )(proj, proj, proj, proj)


def _pattern_count(delta):
    cnt = jnp.zeros(delta.shape, F32)
    for w, d in PATTERNS:
        assert d & (d - 1) == 0
        ok = (delta >= 0) & (delta <= w) & (jnp.bitwise_and(delta, d - 1) == 0)
        cnt = cnt + ok.astype(F32)
    return cnt


def _attn_sample_kernel(q_ref, kn_ref, vn_ref, ga_ref, wk_ref, wv_ref, o_ref):
    tq = q_ref.shape[0]
    wb = wk_ref.shape[0]
    scale = HEAD_DIM ** -0.5
    qi = lax.broadcasted_iota(jnp.int32, (tq, wb), 0)
    kj = lax.broadcasted_iota(jnp.int32, (tq, wb), 1)
    cnt_w = _pattern_count(wb + qi - kj)
    qn = lax.broadcasted_iota(jnp.int32, (tq, tq), 0)
    kn = lax.broadcasted_iota(jnp.int32, (tq, tq), 1)
    cnt_n = _pattern_count(qn - kn)
    for h in range(q_ref.shape[1] // HEAD_DIM):
        cs = slice(h * HEAD_DIM, (h + 1) * HEAD_DIM)
        q = (q_ref[:, cs] * scale).astype(BF16)
        sw = jnp.where(cnt_w > 0, _dot_nt(q, wk_ref[:, cs].astype(BF16)), NEG_INF)
        sn = jnp.where(cnt_n > 0, _dot_nt(q, kn_ref[:, cs].astype(BF16)), NEG_INF)
        m = jnp.maximum(jnp.max(sw, axis=-1, keepdims=True),
                        jnp.max(sn, axis=-1, keepdims=True))
        pw = cnt_w * jnp.exp(sw - m)
        pn = cnt_n * jnp.exp(sn - m)
        den = jnp.sum(pw, axis=-1, keepdims=True) + jnp.sum(pn, axis=-1, keepdims=True)
        num = (_dot(pw.astype(BF16), wv_ref[:, cs].astype(BF16))
               + _dot(pn.astype(BF16), vn_ref[:, cs].astype(BF16)))
        o_ref[:, cs] = num / den * _silu(ga_ref[:, cs])


def _attn_sample(proj, win_k, win_v, li, n_seq, tq):
    wb = win_k.shape[2]
    cw = 512
    nc = D_ATTN // cw
    col = lambda off: pl.BlockSpec((None, tq, cw), lambda s, c, off=off: (0, s, off + c))
    win = pl.BlockSpec((None, None, wb, cw), lambda s, c: (li, s, 0, c))
    return pl.pallas_call(
        _attn_sample_kernel,
        grid=(n_seq, nc),
        in_specs=[col(0), col(nc), col(2 * nc), col(3 * nc), win, win],
        out_specs=pl.BlockSpec((None, tq, cw), lambda s, c: (0, s, c)),
        out_shape=jax.ShapeDtypeStruct((1, n_seq * tq, D_ATTN), F32),
        compiler_params=_params("arbitrary", "arbitrary"),
        name="attn_sample",
    )(proj, proj, proj, proj, win_k, win_v)


def _pool_kernel(u_ref, halo_ref, pre_ref, gp_ref, w_ref, s_ref, o_ref, ext_ref, *, pos0):
    i = pl.program_id(1)
    g = pl.program_id(2)
    tm = u_ref.shape[0]
    npre = POOL_MAX

    @pl.when(i == 0)
    def _():
        ext_ref[0:npre, :] = pre_ref[...]

    @pl.when(i > 0)
    def _():
        ext_ref[0:npre, :] = halo_ref[...]

    u = u_ref[...]
    ext_ref[npre:, :] = u
    pos = pos0 + i * tm + lax.broadcasted_iota(jnp.int32, (tm, 1), 0)

    for gi, w in enumerate(POOL_WINDOWS):
        @pl.when(g == gi)
        def _(w=w):
            win = u
            for j in range(1, w):
                win = win + ext_ref[npre - j:npre - j + tm, :]
            inv_cnt = 1.0 / jnp.minimum(w, pos + 1).astype(F32)
            pooled = win * inv_cnt - u
            mixed = _dot(pooled.astype(BF16), w_ref[...].astype(BF16)) * s_ref[...]
            o_ref[...] = (mixed * _silu(gp_ref[...])).astype(o_ref.dtype)


def _pool(proj, prefix, pool_w, pool_scale, seq_rows, tm, pos0, out_dtype):
    g_, r_, _ = proj.shape
    n_seq = g_ * (r_ // seq_rows)
    nt = seq_rows // tm
    per_g = r_ // seq_rows
    ng = len(POOL_WINDOWS)
    u_blk0 = 4 * D_ATTN // POOL_GROUP
    gp_blk0 = u_blk0 + ng
    hb = tm // POOL_MAX

    def seq_map(blk0):
        return lambda s, i, g: (s // per_g, (s % per_g) * nt + i, blk0 + g)

    def halo_map(s, i, g):
        return (s // per_g, jnp.maximum(((s % per_g) * nt + i) * hb - 1, 0), u_blk0 + g)

    return pl.pallas_call(
        functools.partial(_pool_kernel, pos0=pos0),
        grid=(n_seq, nt, ng),
        in_specs=[
            pl.BlockSpec((None, tm, POOL_GROUP), seq_map(u_blk0)),
            pl.BlockSpec((None, POOL_MAX, POOL_GROUP), halo_map),
            pl.BlockSpec((None, POOL_MAX, POOL_GROUP), lambda s, i, g: (s, 0, g)),
            pl.BlockSpec((None, tm, POOL_GROUP), seq_map(gp_blk0)),
            pl.BlockSpec((None, POOL_GROUP, POOL_GROUP), lambda s, i, g: (g, 0, 0)),
            pl.BlockSpec((1, POOL_GROUP), lambda s, i, g: (0, g)),
        ],
        out_specs=pl.BlockSpec((None, tm, POOL_GROUP), seq_map(0)),
        out_shape=jax.ShapeDtypeStruct((g_, r_, D_POOL), out_dtype),
        scratch_shapes=[pltpu.VMEM((POOL_MAX + tm, POOL_GROUP), F32)],
        compiler_params=_params("arbitrary", "arbitrary", "arbitrary"),
        name="pool_mixer",
    )(proj, proj, prefix, proj, pool_w, pool_scale)


def _final_norm_kernel(x_ref, g_ref, o_ref):
    x = x_ref[...]
    ms = jnp.mean(x * x, axis=-1, keepdims=True)
    o_ref[...] = x * lax.rsqrt(ms + RMS_EPS) * g_ref[...]


def _final_norm(x, g, bb, tm):
    bk, t, d = x.shape
    return pl.pallas_call(
        _final_norm_kernel,
        grid=(bk // bb, t // tm),
        in_specs=[pl.BlockSpec((bb, tm, d), lambda b, i: (b, i, 0)),
                  pl.BlockSpec((1, d), lambda b, i: (0, 0))],
        out_specs=pl.BlockSpec((bb, tm, d), lambda b, i: (b, i, 0)),
        out_shape=jax.ShapeDtypeStruct((bk, t, d), F32),
        compiler_params=_params("arbitrary", "arbitrary"),
        name="final_norm",
    )(x, g)


def _trunk(x, mods, pos0, state, weights, bb, tm):
    (norm_g, w_in_even, w_out_even, pool_w, pool_scale, w_in_odd, conv_w, w_out_odd,
     final_g) = weights
    bk, t, _ = x.shape
    fresh = state is None
    cos, sin = _rope_tables(t, pos0)
    if bb > 1:
        cos, sin = jnp.tile(cos, (bb, 1)), jnp.tile(sin, (bb, 1))
    new_k, new_v, new_pool, new_conv = [], [], [], []
    for layer in range(DEPTH):
        li = layer // 2
        mod = mods[layer]
        g = norm_g[layer][None, :]
        if layer % 2 == 0:
            proj = _in_even(x, mod, g, w_in_even[li], cos, sin, bb, tm)
            seq = proj.reshape(bk, t, D_IN_EVEN)
            k = seq[:, :, D_ATTN:2 * D_ATTN]
            v = seq[:, :, 2 * D_ATTN:3 * D_ATTN]
            u = seq[:, :, 4 * D_ATTN:4 * D_ATTN + D_POOL]
            if fresh:
                keep = min(WINDOW_MAX, t)
                k, v = k[:, t - keep:], v[:, t - keep:]
                pool_prefix = jnp.zeros((bk, POOL_MAX - 1, D_POOL), F32)
                attn = _attn_prompt(proj)
            else:
                win_k, win_v, pool_st, _ = state
                pool_prefix = pool_st[li]
                attn = _attn_sample(proj, win_k, win_v, li, bk, t)
            new_k.append(k.reshape(bk, -1, N_HEADS, HEAD_DIM))
            new_v.append(v.reshape(bk, -1, N_HEADS, HEAD_DIM))
            new_pool.append(jnp.concatenate([pool_prefix, u], axis=1)[:, -(POOL_MAX - 1):])
            prefix = jnp.pad(pool_prefix, ((0, 0), (1, 0), (0, 0)))
            pooled = _pool(proj, prefix, pool_w[li], pool_scale[li][None, :], t,
                           min(t, 1024), pos0, BF16 if fresh else F32)
            x = _out_proj(attn, 0, pooled, 0, w_out_even[li], x, mod, bb, tm)
        else:
            if fresh:
                prefix = jnp.zeros((bk, SUBLANES, D_CONV), F32)
            else:
                prefix = jnp.pad(state[3][li], ((0, 0), (SUBLANES - (CONV_WIDTH - 1), 0), (0, 0)))
            y, tail = _in_odd(x, mod, g, w_in_odd[li], conv_w[li], prefix, bb, tm)
            new_conv.append(tail[:, SUBLANES - (CONV_WIDTH - 1):])
            x = _out_proj(y, 0, y, 1, w_out_odd[li], x, mod, bb, tm)
    y = _final_norm(x, final_g[None, :], bb, tm)
    return y, jnp.stack(new_k), jnp.stack(new_v), jnp.stack(new_pool), jnp.stack(new_conv)


def kernel(x_prompt, x_sample, c_prompt, c_sample, state_win_k, state_win_v, state_pool,
           state_conv, ada_w, ada_b, norm_g, w_in_even, w_out_even, pool_w, pool_scale,
           w_in_odd, conv_w, w_out_odd, final_g):
    nb, t_p, _ = x_prompt.shape
    ns, t_s, _ = x_sample.shape
    n_c = nb + ns
    c_rows = -(-n_c // SUBLANES) * SUBLANES
    c_all = jnp.pad(jnp.concatenate([c_prompt, c_sample], axis=0), ((0, c_rows - n_c), (0, 0)))
    mods = _ada_modulation(c_all, ada_w, ada_b)[:, :, None, :]
    weights = (norm_g, w_in_even.astype(BF16), w_out_even.astype(BF16), pool_w, pool_scale,
               w_in_odd.astype(BF16), conv_w, w_out_odd.astype(BF16), final_g)

    y_p, kp, vp, poolp, convp = _trunk(x_prompt, mods[:, :nb], 0, None, weights, 1, 1024)

    wb = state_win_k.shape[2]
    state = (state_win_k.reshape(-1, ns, wb, D_ATTN), state_win_v.reshape(-1, ns, wb, D_ATTN),
             state_pool, state_conv)
    y_s, ksm, vsm, pools, convs = _trunk(x_sample, mods[:, nb:n_c], PAST_LEN, state, weights,
                                         ns, t_s)
    return (y_p, y_s, kp, vp, poolp, convp, ksm, vsm, pools, convs)
```

```python
import functools
import math

import jax
import jax.numpy as jnp
from jax import lax
from jax.experimental import pallas as pl
from jax.experimental.pallas import tpu as pltpu

F32 = jnp.float32
BF16 = jnp.bfloat16

D_MODEL = 2048
DEPTH = 4
PAST_LEN = 16384
D_ATTN = D_MODEL // 2
HEAD_DIM = 128
N_HEADS = D_ATTN // HEAD_DIM
PATTERNS = ((128, 1), (512, 4), (2048, 16))
KEYS_PER_PATTERN = 128
ATTN_UNROLL = 8
SAMPLE_GROUP = max(d for _, d in PATTERNS)
WINDOW_MAX = max(w for w, _ in PATTERNS)
ROPE_THETA = 10000.0
D_POOL = D_MODEL // 2
POOL_WINDOWS = (2, 4, 8, 16)
POOL_GROUP = D_POOL // len(POOL_WINDOWS)
POOL_MAX = max(POOL_WINDOWS)
D_CONV = D_MODEL
CONV_WIDTH = 3
D_IN_EVEN = 4 * D_ATTN + 2 * D_POOL
RMS_EPS = 1e-6
NEG_INF = -1e30
QK_SCALE = HEAD_DIM ** -0.5 * math.log2(math.e)

LANES = 128
SUBLANES = 8
VMEM_LIMIT = 48 * 1024 * 1024


def _params(*semantics):
    return pltpu.CompilerParams(dimension_semantics=semantics,
                                vmem_limit_bytes=VMEM_LIMIT)


def _silu(v):
    return v / (1.0 + jnp.exp(-v))


def _dot(a, b):
    return jnp.dot(a, b, preferred_element_type=F32)


def _dot_nt(a, b):
    return lax.dot_general(a, b, (((1,), (1,)), ((), ())), preferred_element_type=F32)


def _rope_kernel(inv_ref, cos_ref, sin_ref, *, pos0):
    t = cos_ref.shape[0]
    pos = (lax.broadcasted_iota(jnp.int32, (t, HEAD_DIM), 0) + pos0).astype(F32)
    lane = lax.broadcasted_iota(jnp.int32, (t, HEAD_DIM), 1)
    ang = pos * inv_ref[...]
    s = jnp.sin(ang)
    cos_ref[...] = jnp.cos(ang)
    sin_ref[...] = jnp.where(lane < HEAD_DIM // 2, -s, s)


def _rope_tables(t, pos0):
    half = HEAD_DIM // 2
    inv = jnp.power(ROPE_THETA, -jnp.arange(half, dtype=F32) / half)
    inv = jnp.concatenate([inv, inv])[None, :]
    return pl.pallas_call(
        functools.partial(_rope_kernel, pos0=pos0),
        out_shape=(jax.ShapeDtypeStruct((t, HEAD_DIM), F32),
                   jax.ShapeDtypeStruct((t, HEAD_DIM), F32)),
        name="rope_tables",
    )(inv)


def _ada_kernel(c_ref, w_ref, b_ref, o_ref):
    a = _silu(c_ref[...]).astype(BF16)
    o_ref[...] = _dot(a, w_ref[...].astype(BF16)) + b_ref[...]


def _ada_modulation(c_all, ada_w, ada_b):
    rows = c_all.shape[0]
    n_out = ada_w.shape[2]
    tn = 1024
    return pl.pallas_call(
        _ada_kernel,
        grid=(DEPTH, n_out // tn),
        in_specs=[
            pl.BlockSpec((rows, D_MODEL), lambda l, n: (0, 0)),
            pl.BlockSpec((None, D_MODEL, tn), lambda l, n: (l, 0, n)),
            pl.BlockSpec((None, 1, tn), lambda l, n: (l, 0, n)),
        ],
        out_specs=pl.BlockSpec((None, rows, tn), lambda l, n: (l, 0, n)),
        out_shape=jax.ShapeDtypeStruct((DEPTH, rows, n_out), F32),
        compiler_params=_params("arbitrary", "arbitrary"),
        name="ada_modulation",
    )(c_all, ada_w, ada_b.reshape(DEPTH, 1, n_out))


def _modulated_norm(x_ref, shift_ref, scale_ref, g_ref):
    x = x_ref[...]
    ms = jnp.mean(x * x, axis=-1, keepdims=True)
    y = x * lax.rsqrt(ms + RMS_EPS) * g_ref[...]
    h = y * (1.0 + scale_ref[...]) + shift_ref[...]
    bb, tm, d = h.shape
    return h.reshape(bb * tm, d).astype(BF16)


def _in_even_kernel(x_ref, shift_ref, scale_ref, g_ref, w_ref, cos_ref, sin_ref,
                    o_ref, h_ref):
    n = pl.program_id(2)
    tn = w_ref.shape[1]
    per = D_ATTN // tn
    assert D_POOL == D_ATTN and D_ATTN % tn == 0

    @pl.when(n == 0)
    def _():
        h_ref[...] = _modulated_norm(x_ref, shift_ref, scale_ref, g_ref)

    acc = _dot(h_ref[...], w_ref[...])

    def rotary(out_scale):
        cos = cos_ref[...] * out_scale
        sin = sin_ref[...] * out_scale
        for j in range(tn // HEAD_DIM):
            a = acc[:, j * HEAD_DIM:(j + 1) * HEAD_DIM]
            o_ref[:, j * HEAD_DIM:(j + 1) * HEAD_DIM] = (
                a * cos + pltpu.roll(a, HEAD_DIM // 2, axis=1) * sin)

    @pl.when(n < per)
    def _():
        rotary(QK_SCALE)

    @pl.when((n >= per) & (n < 2 * per))
    def _():
        rotary(1.0)

    is_gate = ((n >= 3 * per) & (n < 4 * per)) | (n >= 5 * per)

    @pl.when(is_gate)
    def _():
        o_ref[...] = _silu(acc)

    @pl.when((n >= 2 * per) & jnp.logical_not(is_gate))
    def _():
        o_ref[...] = acc


def _in_even(x, mod, g, w, cos_rows, sin_rows, bb, tm):
    bk, t, d = x.shape
    n_out = w.shape[1]
    tn = 512
    rows = bb * tm
    grid = (bk // bb, t // tm, n_out // tn)
    return pl.pallas_call(
        _in_even_kernel,
        grid=grid,
        in_specs=[
            pl.BlockSpec((bb, tm, d), lambda b, i, n: (b, i, 0)),
            pl.BlockSpec((bb, 1, d), lambda b, i, n: (b, 0, 0)),
            pl.BlockSpec((bb, 1, d), lambda b, i, n: (b, 0, 1)),
            pl.BlockSpec((1, d), lambda b, i, n: (0, 0)),
            pl.BlockSpec((d, tn), lambda b, i, n: (0, n)),
            pl.BlockSpec((rows, HEAD_DIM), lambda b, i, n: (i, 0)),
            pl.BlockSpec((rows, HEAD_DIM), lambda b, i, n: (i, 0)),
        ],
        out_specs=pl.BlockSpec((None, rows, tn), lambda b, i, n: (b, i, n)),
        out_shape=jax.ShapeDtypeStruct((bk // bb, (t // tm) * rows, n_out), F32),
        scratch_shapes=[pltpu.VMEM((rows, d), BF16)],
        compiler_params=_params("arbitrary", "arbitrary", "arbitrary"),
        name="in_proj_even",
    )(x, mod, mod, g, w, cos_rows, sin_rows)


def _in_odd_kernel(x_ref, shift_ref, scale_ref, g_ref, wb_ref, wc_ref, wh_ref, wg_ref,
                   cw_ref, pre_ref, o_ref, tail_ref, h_ref, zbuf_ref):
    i = pl.program_id(1)
    n = pl.program_id(2)
    bb, tm, _ = x_ref.shape
    tn = wb_ref.shape[1]
    halo = SUBLANES
    tail_cols = pl.ds(pl.multiple_of(n * tn, tn), tn)

    @pl.when(n == 0)
    def _():
        h_ref[...] = _modulated_norm(x_ref, shift_ref, scale_ref, g_ref)

    h = h_ref[...]
    z = (_dot(h, wc_ref[...]) * _dot(h, wh_ref[...])).reshape(bb, tm, tn)

    @pl.when(i == 0)
    def _():
        zbuf_ref[:, 0:halo, :] = pre_ref[...]

    @pl.when(i > 0)
    def _():
        zbuf_ref[:, 0:halo, :] = tail_ref[:, :, tail_cols]

    zbuf_ref[:, halo:, :] = z
    cw = cw_ref[...]
    yc = zbuf_ref[:, halo - 2:halo - 2 + tm, :] * cw[0:1]
    yc = yc + zbuf_ref[:, halo - 1:halo - 1 + tm, :] * cw[1:2]
    yc = yc + z * cw[2:3]
    tail_ref[:, :, tail_cols] = zbuf_ref[:, tm:tm + halo, :]

    bg = _dot(h, wb_ref[...]).reshape(bb, tm, tn)
    gate = _dot(h, wg_ref[...]).reshape(bb, tm, tn)
    y = bg * yc * _silu(gate)
    o_ref[...] = y.reshape(bb * tm, tn).astype(BF16)


def _in_odd(x, mod, g, w, conv_w, prefix, bb, tm):
    bk, t, d = x.shape
    tn = 256
    nt = D_CONV // tn
    rows = bb * tm
    grid = (bk // bb, t // tm, nt)
    w_spec = lambda k: pl.BlockSpec((d, tn), lambda b, i, n, k=k: (0, k * nt + n))
    return pl.pallas_call(
        _in_odd_kernel,
        grid=grid,
        in_specs=[
            pl.BlockSpec((bb, tm, d), lambda b, i, n: (b, i, 0)),
            pl.BlockSpec((bb, 1, d), lambda b, i, n: (b, 0, 0)),
            pl.BlockSpec((bb, 1, d), lambda b, i, n: (b, 0, 1)),
            pl.BlockSpec((1, d), lambda b, i, n: (0, 0)),
            w_spec(0), w_spec(1), w_spec(2), w_spec(3),
            pl.BlockSpec((CONV_WIDTH, tn), lambda b, i, n: (0, n)),
            pl.BlockSpec((bb, SUBLANES, tn), lambda b, i, n: (b, 0, n)),
        ],
        out_specs=(
            pl.BlockSpec((None, rows, tn), lambda b, i, n: (b, i, n)),
            pl.BlockSpec((bb, SUBLANES, D_CONV), lambda b, i, n: (b, 0, 0)),
        ),
        out_shape=(
            jax.ShapeDtypeStruct((bk // bb, (t // tm) * rows, D_CONV), BF16),
            jax.ShapeDtypeStruct((bk, SUBLANES, D_CONV), F32),
        ),
        scratch_shapes=[
            pltpu.VMEM((rows, d), BF16),
            pltpu.VMEM((bb, SUBLANES + tm, tn), F32),
        ],
        compiler_params=_params("arbitrary", "arbitrary", "arbitrary"),
        name="in_proj_odd_conv",
    )(x, mod, mod, g, w, w, w, w, conv_w, prefix)


def _out_proj_kernel(a1_ref, a2_ref, w1_ref, w2_ref, x_ref, gate_ref, o_ref):
    bb, tm, tn = x_ref.shape
    acc = _dot(a1_ref[...].astype(BF16), w1_ref[...])
    acc = acc + _dot(a2_ref[...].astype(BF16), w2_ref[...])
    o_ref[...] = x_ref[...] + gate_ref[...] * acc.reshape(bb, tm, tn)


def _out_proj(a1, a1_blk, a2, a2_blk, w, x, mod, bb, tm):
    bk, t, d = x.shape
    kh = w.shape[0] // 2
    tn = 512
    rows = bb * tm
    gate_blk0 = 2 * d // tn
    return pl.pallas_call(
        _out_proj_kernel,
        grid=(bk // bb, t // tm, d // tn),
        in_specs=[
            pl.BlockSpec((None, rows, kh), lambda b, i, n: (b, i, a1_blk)),
            pl.BlockSpec((None, rows, kh), lambda b, i, n: (b, i, a2_blk)),
            pl.BlockSpec((kh, tn), lambda b, i, n: (0, n)),
            pl.BlockSpec((kh, tn), lambda b, i, n: (1, n)),
            pl.BlockSpec((bb, tm, tn), lambda b, i, n: (b, i, n)),
            pl.BlockSpec((bb, 1, tn), lambda b, i, n: (b, 0, gate_blk0 + n)),
        ],
        out_specs=pl.BlockSpec((bb, tm, tn), lambda b, i, n: (b, i, n)),
        out_shape=jax.ShapeDtypeStruct((bk, t, d), F32),
        compiler_params=_params("arbitrary", "arbitrary", "arbitrary"),
        name="out_proj_residual",
    )(a1, a2, w, w, x, mod)


def _attn_prompt_kernel(q_ref, k_ref, v_ref, ga_ref, bias_ref, o_ref, acc_ref, m_ref, l_ref):
    t = q_ref.shape[0]
    bq = KEYS_PER_PATTERN
    dilations = sorted((d for _, d in PATTERNS), reverse=True)
    assert dilations[-1] == 1

    for pi, d in enumerate(dilations):
        span = bq * d
        nblk = t // span
        first = pi == 0
        last = pi == len(dilations) - 1

        def rows(ref, start, n, d=d):
            if d == 1:
                return ref[pl.ds(pl.multiple_of(start, bq), n), :]
            return ref[pl.ds(start, n, stride=d), :]

        def put(ref, start, val, d=d):
            if d == 1:
                ref[pl.ds(pl.multiple_of(start, bq), bq), :] = val
            else:
                ref[pl.ds(start, bq, stride=d), :] = val

        def body(it, carry, nblk=nblk, span=span, rows=rows, put=put, first=first, last=last):
            bases, kbases, has_prevs = [], [], []
            for u in range(ATTN_UNROLL):
                itu = it * ATTN_UNROLL + u
                blk = itu % nblk
                base = itu // nblk + blk * span
                has_prev = blk > 0
                bases.append(base)
                kbases.append(jnp.where(has_prev, base - span, base))
                has_prevs.append(has_prev.astype(jnp.int32))
            stack = lambda ref, starts, n: jnp.stack([rows(ref, s, n) for s in starts])

            def keys(ref):
                if nblk == 2 and ATTN_UNROLL % 2 == 0:
                    pairs = [rows(ref, s, 2 * bq).astype(BF16) for s in kbases[::2]]
                    return jnp.stack([pairs[u // 2] for u in range(ATTN_UNROLL)])
                return stack(ref, kbases, 2 * bq).astype(BF16)

            q = stack(q_ref, bases, bq).astype(BF16)
            k = keys(k_ref)
            v = keys(v_ref)
            s = jnp.einsum('uqd,ukd->uqk', q, k, preferred_element_type=F32)
            s = s + jnp.stack([bias_ref[hp] for hp in has_prevs])
            m = jnp.max(jnp.maximum(s[:, :, :bq], s[:, :, bq:]), axis=-1, keepdims=True)
            p = jnp.exp2(s - m)
            l = jnp.sum(p[:, :, :bq] + p[:, :, bq:], axis=-1, keepdims=True)
            num = jnp.einsum('uqk,ukd->uqd', p.astype(BF16), v, preferred_element_type=F32)
            m = jnp.broadcast_to(m, num.shape)
            l = jnp.broadcast_to(l, num.shape)
            if not first:
                m_old = stack(m_ref, bases, bq)
                m_new = jnp.maximum(m_old, m)
                a_old = jnp.exp2(m_old - m_new)
                a_new = jnp.exp2(m - m_new)
                num = a_old * stack(acc_ref, bases, bq) + a_new * num
                l = a_old * stack(l_ref, bases, bq) + a_new * l
                m = m_new
            for u, base in enumerate(bases):
                if last:
                    rows_out = pl.ds(pl.multiple_of(base, bq), bq)
                    o_ref[rows_out, :] = (num[u] / l[u] * ga_ref[rows_out, :]).astype(BF16)
                else:
                    put(acc_ref, base, num[u])
                    put(m_ref, base, m[u])
                    put(l_ref, base, l[u])
            return carry

        lax.fori_loop(0, t // (bq * ATTN_UNROLL), body, 0)


def _attn_prompt(proj):
    b, t, _ = proj.shape
    for w, d in PATTERNS:
        assert w // d == KEYS_PER_PATTERN and t % (2 * KEYS_PER_PATTERN * d) == 0
    assert (t // KEYS_PER_PATTERN) % ATTN_UNROLL == 0
    col = lambda off: pl.BlockSpec((None, t, HEAD_DIM), lambda bi, h, off=off: (bi, 0, off + h))
    bq = KEYS_PER_PATTERN
    qi = jnp.arange(bq, dtype=jnp.int32)[:, None]
    kj = jnp.arange(2 * bq, dtype=jnp.int32)[None, :]
    visible = jnp.stack([kj <= qi, (kj >= qi) & (kj <= qi + bq)])
    bias = jnp.where(visible, 0.0, NEG_INF).astype(F32)
    return pl.pallas_call(
        _attn_prompt_kernel,
        grid=(b, N_HEADS),
        in_specs=[col(0), col(N_HEADS), col(2 * N_HEADS), col(3 * N_HEADS),
                  pl.BlockSpec((2, bq, 2 * bq), lambda bi, h: (0, 0, 0))],
        out_specs=pl.BlockSpec((None, t, HEAD_DIM), lambda bi, h: (bi, 0, h)),
        out_shape=jax.ShapeDtypeStruct((b, t, D_ATTN), BF16),
        scratch_shapes=[pltpu.VMEM((t, HEAD_DIM), F32)] * 3,
        compiler_params=_params("arbitrary", "arbitrary"),
        name="attn_prompt",
    )(proj, proj, proj, proj, bias)


def _pattern_count(delta):
    cnt = jnp.zeros(delta.shape, F32)
    for w, d in PATTERNS:
        assert d & (d - 1) == 0
        ok = (delta >= 0) & (delta <= w) & (jnp.bitwise_and(delta, d - 1) == 0)
        cnt = cnt + ok.astype(F32)
    return cnt


def _attn_sample_kernel(q_ref, kn_ref, vn_ref, ga_ref, ka_ref, kb_ref, va_ref, vb_ref, o_ref,
                        *, wb):
    tq = q_ref.shape[0]
    nh = N_HEADS
    assert tq & (tq - 1) == 0 and nh & (nh - 1) == 0
    tq_bits, nh_bits, sub_bits = (n.bit_length() - 1 for n in (tq, nh, SUBLANES))
    nq = nh * tq
    scale = HEAD_DIM ** -0.5

    def heads_to_rows(ref):
        return jnp.concatenate(
            [ref[:, h * HEAD_DIM:(h + 1) * HEAD_DIM] for h in range(nh)], axis=0)

    def flat(ref):
        g, j, h, d = ref.shape
        return ref[...].reshape(g * j * h, d).astype(BF16)

    def window_weights(n_groups, group0, residue0):
        shape = (nq, n_groups * SUBLANES * nh)
        r = lax.broadcasted_iota(jnp.int32, shape, 0)
        c = lax.broadcasted_iota(jnp.int32, shape, 1)
        key_head = c & (nh - 1)
        residue = residue0 + ((c >> nh_bits) & (SUBLANES - 1))
        key_row = (group0 + (c >> (nh_bits + sub_bits))) * SAMPLE_GROUP + residue
        delta = wb + (r & (tq - 1)) - key_row
        return jnp.where((r >> tq_bits) == key_head, _pattern_count(delta), 0.0)

    r = lax.broadcasted_iota(jnp.int32, (nq, nq), 0)
    c = lax.broadcasted_iota(jnp.int32, (nq, nq), 1)
    w_new = jnp.where((r >> tq_bits) == (c >> tq_bits),
                      _pattern_count((r & (tq - 1)) - (c & (tq - 1))), 0.0)
    n_dense = kb_ref.shape[0]
    parts = (
        (flat(ka_ref), flat(va_ref), window_weights(ka_ref.shape[0], 0, 0)),
        (flat(kb_ref), flat(vb_ref),
         window_weights(n_dense, wb // SAMPLE_GROUP - n_dense, SUBLANES)),
        (heads_to_rows(kn_ref).astype(BF16), heads_to_rows(vn_ref).astype(BF16), w_new),
    )
    q = heads_to_rows(q_ref).astype(BF16)
    scores = [jnp.where(wgt > 0, _dot_nt(q, k), NEG_INF) for k, _, wgt in parts]
    m = functools.reduce(jnp.maximum, [jnp.max(s, axis=-1, keepdims=True) for s in scores])
    den = jnp.zeros((nq, 1), F32)
    num = jnp.zeros((nq, HEAD_DIM), F32)
    for s, (_, v, wgt) in zip(scores, parts):
        p = wgt * jnp.exp2(s - m)
        den = den + jnp.sum(p, axis=-1, keepdims=True)
        num = num + _dot(p.astype(BF16), v)
    out = num / den
    for h in range(nh):
        cs = slice(h * HEAD_DIM, (h + 1) * HEAD_DIM)
        o_ref[:, cs] = out[h * tq:(h + 1) * tq] * ga_ref[:, cs]


def _attn_sample(proj, win_k, win_v, li, n_seq, tq):
    wb = win_k.shape[2]
    dense_reach = max(w for w, d in PATTERNS if d < SAMPLE_GROUP)
    assert max(d for _, d in PATTERNS) == SAMPLE_GROUP == 2 * SUBLANES and tq == SUBLANES
    assert wb % SAMPLE_GROUP == 0 and dense_reach % SAMPLE_GROUP == 0 and wb % dense_reach == 0
    n_groups = wb // SAMPLE_GROUP
    n_dense = dense_reach // SAMPLE_GROUP
    win_k = win_k.reshape(-1, n_seq, n_groups, SAMPLE_GROUP, N_HEADS, HEAD_DIM)
    win_v = win_v.reshape(-1, n_seq, n_groups, SAMPLE_GROUP, N_HEADS, HEAD_DIM)
    col = lambda blk: pl.BlockSpec((None, tq, D_ATTN), lambda s, blk=blk: (0, s, blk))
    sparse = pl.BlockSpec((None, None, n_groups, SUBLANES, N_HEADS, HEAD_DIM),
                          lambda s: (li, s, 0, 0, 0, 0))
    dense = pl.BlockSpec((None, None, n_dense, SUBLANES, N_HEADS, HEAD_DIM),
                         lambda s: (li, s, n_groups // n_dense - 1, 1, 0, 0))
    return pl.pallas_call(
        functools.partial(_attn_sample_kernel, wb=wb),
        grid=(n_seq,),
        in_specs=[col(0), col(1), col(2), col(3), sparse, dense, sparse, dense],
        out_specs=pl.BlockSpec((None, tq, D_ATTN), lambda s: (0, s, 0)),
        out_shape=jax.ShapeDtypeStruct((1, n_seq * tq, D_ATTN), F32),
        compiler_params=_params("arbitrary"),
        name="attn_sample",
    )(proj, proj, proj, proj, win_k, win_k, win_v, win_v)


def _pool_kernel(u_ref, halo_ref, pre_ref, gp_ref, w_ref, s_ref, o_ref, ext_ref, *, pos0):
    i = pl.program_id(1)
    g = pl.program_id(2)
    tm = u_ref.shape[0]
    npre = POOL_MAX

    @pl.when(i == 0)
    def _():
        ext_ref[0:npre, :] = pre_ref[...]

    @pl.when(i > 0)
    def _():
        ext_ref[0:npre, :] = halo_ref[...]

    u = u_ref[...]
    ext_ref[npre:, :] = u
    pos = pos0 + i * tm + lax.broadcasted_iota(jnp.int32, (tm, 1), 0)

    for gi, w in enumerate(POOL_WINDOWS):
        @pl.when(g == gi)
        def _(w=w):
            win = u
            for j in range(1, w):
                win = win + ext_ref[npre - j:npre - j + tm, :]
            inv_cnt = 1.0 / jnp.minimum(w, pos + 1).astype(F32)
            pooled = win * inv_cnt - u
            mixed = _dot(pooled.astype(BF16), w_ref[...].astype(BF16)) * s_ref[...]
            o_ref[...] = (mixed * gp_ref[...]).astype(o_ref.dtype)


def _pool(proj, prefix, pool_w, pool_scale, seq_rows, tm, pos0, out_dtype):
    g_, r_, _ = proj.shape
    n_seq = g_ * (r_ // seq_rows)
    nt = seq_rows // tm
    per_g = r_ // seq_rows
    ng = len(POOL_WINDOWS)
    u_blk0 = 4 * D_ATTN // POOL_GROUP
    gp_blk0 = u_blk0 + ng
    hb = tm // POOL_MAX

    def seq_map(blk0):
        return lambda s, i, g: (s // per_g, (s % per_g) * nt + i, blk0 + g)

    def halo_map(s, i, g):
        return (s // per_g, jnp.maximum(((s % per_g) * nt + i) * hb - 1, 0), u_blk0 + g)

    return pl.pallas_call(
        functools.partial(_pool_kernel, pos0=pos0),
        grid=(n_seq, nt, ng),
        in_specs=[
            pl.BlockSpec((None, tm, POOL_GROUP), seq_map(u_blk0)),
            pl.BlockSpec((None, POOL_MAX, POOL_GROUP), halo_map),
            pl.BlockSpec((None, POOL_MAX, POOL_GROUP), lambda s, i, g: (s, 0, g)),
            pl.BlockSpec((None, tm, POOL_GROUP), seq_map(gp_blk0)),
            pl.BlockSpec((None, POOL_GROUP, POOL_GROUP), lambda s, i, g: (g, 0, 0)),
            pl.BlockSpec((1, POOL_GROUP), lambda s, i, g: (0, g)),
        ],
        out_specs=pl.BlockSpec((None, tm, POOL_GROUP), seq_map(0)),
        out_shape=jax.ShapeDtypeStruct((g_, r_, D_POOL), out_dtype),
        scratch_shapes=[pltpu.VMEM((POOL_MAX + tm, POOL_GROUP), F32)],
        compiler_params=_params("arbitrary", "arbitrary", "arbitrary"),
        name="pool_mixer",
    )(proj, proj, prefix, proj, pool_w, pool_scale)


def _final_norm_kernel(x_ref, g_ref, o_ref):
    x = x_ref[...]
    ms = jnp.mean(x * x, axis=-1, keepdims=True)
    o_ref[...] = x * lax.rsqrt(ms + RMS_EPS) * g_ref[...]


def _final_norm(x, g, bb, tm):
    bk, t, d = x.shape
    return pl.pallas_call(
        _final_norm_kernel,
        grid=(bk // bb, t // tm),
        in_specs=[pl.BlockSpec((bb, tm, d), lambda b, i: (b, i, 0)),
                  pl.BlockSpec((1, d), lambda b, i: (0, 0))],
        out_specs=pl.BlockSpec((bb, tm, d), lambda b, i: (b, i, 0)),
        out_shape=jax.ShapeDtypeStruct((bk, t, d), F32),
        compiler_params=_params("arbitrary", "arbitrary"),
        name="final_norm",
    )(x, g)


def _trunk(x, mods, pos0, state, weights, bb, tm):
    (norm_g, w_in_even, w_out_even, pool_w, pool_scale, w_in_odd, conv_w, w_out_odd,
     final_g) = weights
    bk, t, _ = x.shape
    fresh = state is None
    cos, sin = _rope_tables(t, pos0)
    if bb > 1:
        cos, sin = jnp.tile(cos, (bb, 1)), jnp.tile(sin, (bb, 1))
    new_k, new_v, new_pool, new_conv = [], [], [], []
    for layer in range(DEPTH):
        li = layer // 2
        mod = mods[layer]
        g = norm_g[layer][None, :]
        if layer % 2 == 0:
            proj = _in_even(x, mod, g, w_in_even[li], cos, sin, bb, tm)
            seq = proj.reshape(bk, t, D_IN_EVEN)
            k = seq[:, :, D_ATTN:2 * D_ATTN]
            v = seq[:, :, 2 * D_ATTN:3 * D_ATTN]
            u = seq[:, :, 4 * D_ATTN:4 * D_ATTN + D_POOL]
            if fresh:
                keep = min(WINDOW_MAX, t)
                k, v = k[:, t - keep:], v[:, t - keep:]
                pool_prefix = jnp.zeros((bk, POOL_MAX - 1, D_POOL), F32)
                attn = _attn_prompt(proj)
            else:
                win_k, win_v, pool_st, _ = state
                pool_prefix = pool_st[li]
                attn = _attn_sample(proj, win_k, win_v, li, bk, t)
            new_k.append(k.reshape(bk, -1, N_HEADS, HEAD_DIM))
            new_v.append(v.reshape(bk, -1, N_HEADS, HEAD_DIM))
            new_pool.append(jnp.concatenate([pool_prefix, u], axis=1)[:, -(POOL_MAX - 1):])
            prefix = jnp.pad(pool_prefix, ((0, 0), (1, 0), (0, 0)))
            pooled = _pool(proj, prefix, pool_w[li], pool_scale[li][None, :], t,
                           min(t, 1024), pos0, BF16 if fresh else F32)
            x = _out_proj(attn, 0, pooled, 0, w_out_even[li], x, mod, bb, tm)
        else:
            if fresh:
                prefix = jnp.zeros((bk, SUBLANES, D_CONV), F32)
            else:
                prefix = jnp.pad(state[3][li], ((0, 0), (SUBLANES - (CONV_WIDTH - 1), 0), (0, 0)))
            y, tail = _in_odd(x, mod, g, w_in_odd[li], conv_w[li], prefix, bb, tm)
            new_conv.append(tail[:, SUBLANES - (CONV_WIDTH - 1):])
            x = _out_proj(y, 0, y, 1, w_out_odd[li], x, mod, bb, tm)
    y = _final_norm(x, final_g[None, :], bb, tm)
    return y, jnp.stack(new_k), jnp.stack(new_v), jnp.stack(new_pool), jnp.stack(new_conv)


def kernel(x_prompt, x_sample, c_prompt, c_sample, state_win_k, state_win_v, state_pool,
           state_conv, ada_w, ada_b, norm_g, w_in_even, w_out_even, pool_w, pool_scale,
           w_in_odd, conv_w, w_out_odd, final_g):
    nb, t_p, _ = x_prompt.shape
    ns, t_s, _ = x_sample.shape
    n_c = nb + ns
    c_rows = -(-n_c // SUBLANES) * SUBLANES
    c_all = jnp.pad(jnp.concatenate([c_prompt, c_sample], axis=0), ((0, c_rows - n_c), (0, 0)))
    mods = _ada_modulation(c_all, ada_w, ada_b)[:, :, None, :]
    weights = (norm_g, w_in_even.astype(BF16), w_out_even.astype(BF16), pool_w, pool_scale,
               w_in_odd.astype(BF16), conv_w, w_out_odd.astype(BF16), final_g)

    y_p, kp, vp, poolp, convp = _trunk(x_prompt, mods[:, :nb], 0, None, weights, 1, 1024)

    state = (state_win_k, state_win_v, state_pool, state_conv)
    y_s, ksm, vsm, pools, convs = _trunk(x_sample, mods[:, nb:n_c], PAST_LEN, state, weights,
                                         ns, t_s)
    return (y_p, y_s, kp, vp, poolp, convp, ksm, vsm, pools, convs)
```

```python
import functools
import math

import jax
import jax.numpy as jnp
from jax import lax
from jax.experimental import pallas as pl
from jax.experimental.pallas import tpu as pltpu

F32 = jnp.float32
BF16 = jnp.bfloat16

D_MODEL = 2048
DEPTH = 4
PAST_LEN = 16384
D_ATTN = D_MODEL // 2
HEAD_DIM = 128
N_HEADS = D_ATTN // HEAD_DIM
PATTERNS = ((128, 1), (512, 4), (2048, 16))
KEYS_PER_PATTERN = 128
ATTN_UNROLL = 8
SAMPLE_GROUP = max(d for _, d in PATTERNS)
WINDOW_MAX = max(w for w, _ in PATTERNS)
ROPE_THETA = 10000.0
D_POOL = D_MODEL // 2
POOL_WINDOWS = (2, 4, 8, 16)
POOL_GROUP = D_POOL // len(POOL_WINDOWS)
POOL_MAX = max(POOL_WINDOWS)
D_CONV = D_MODEL
CONV_WIDTH = 3
D_IN_EVEN = 4 * D_ATTN + 2 * D_POOL
RMS_EPS = 1e-6
NEG_INF = -1e30
QK_SCALE = HEAD_DIM ** -0.5 * math.log2(math.e)

LANES = 128
SUBLANES = 8
VMEM_LIMIT = 48 * 1024 * 1024


def _params(*semantics):
    return pltpu.CompilerParams(dimension_semantics=semantics,
                                vmem_limit_bytes=VMEM_LIMIT)


def _silu(v):
    return v / (1.0 + jnp.exp(-v))


def _dot(a, b):
    return jnp.dot(a, b, preferred_element_type=F32)


def _dot_nt(a, b):
    return lax.dot_general(a, b, (((1,), (1,)), ((), ())), preferred_element_type=F32)


def _rope_kernel(inv_ref, cos_ref, sin_ref, *, pos0):
    t = cos_ref.shape[0]
    pos = (lax.broadcasted_iota(jnp.int32, (t, HEAD_DIM), 0) + pos0).astype(F32)
    lane = lax.broadcasted_iota(jnp.int32, (t, HEAD_DIM), 1)
    ang = pos * inv_ref[...]
    s = jnp.sin(ang)
    cos_ref[...] = jnp.cos(ang)
    sin_ref[...] = jnp.where(lane < HEAD_DIM // 2, -s, s)


def _rope_tables(t, pos0):
    half = HEAD_DIM // 2
    inv = jnp.power(ROPE_THETA, -jnp.arange(half, dtype=F32) / half)
    inv = jnp.concatenate([inv, inv])[None, :]
    return pl.pallas_call(
        functools.partial(_rope_kernel, pos0=pos0),
        out_shape=(jax.ShapeDtypeStruct((t, HEAD_DIM), F32),
                   jax.ShapeDtypeStruct((t, HEAD_DIM), F32)),
        name="rope_tables",
    )(inv)


def _ada_kernel(c_ref, w_ref, b_ref, o_ref):
    a = _silu(c_ref[...]).astype(BF16)
    o_ref[...] = _dot(a, w_ref[...].astype(BF16)) + b_ref[...]


def _ada_modulation(c_all, ada_w, ada_b):
    rows = c_all.shape[0]
    n_out = ada_w.shape[2]
    tn = 1024
    return pl.pallas_call(
        _ada_kernel,
        grid=(DEPTH, n_out // tn),
        in_specs=[
            pl.BlockSpec((rows, D_MODEL), lambda l, n: (0, 0)),
            pl.BlockSpec((None, D_MODEL, tn), lambda l, n: (l, 0, n)),
            pl.BlockSpec((None, 1, tn), lambda l, n: (l, 0, n)),
        ],
        out_specs=pl.BlockSpec((None, rows, tn), lambda l, n: (l, 0, n)),
        out_shape=jax.ShapeDtypeStruct((DEPTH, rows, n_out), F32),
        compiler_params=_params("arbitrary", "arbitrary"),
        name="ada_modulation",
    )(c_all, ada_w, ada_b.reshape(DEPTH, 1, n_out))


def _modulated_norm(x_ref, shift_ref, scale_ref, g_ref):
    x = x_ref[...]
    ms = jnp.mean(x * x, axis=-1, keepdims=True)
    y = x * lax.rsqrt(ms + RMS_EPS) * g_ref[...]
    h = y * (1.0 + scale_ref[...]) + shift_ref[...]
    bb, tm, d = h.shape
    return h.reshape(bb * tm, d).astype(BF16)


def _in_even_kernel(x_ref, shift_ref, scale_ref, g_ref, w_ref, cos_ref, sin_ref,
                    o_ref, h_ref):
    n = pl.program_id(2)
    tn = w_ref.shape[1]
    per = D_ATTN // tn
    assert D_POOL == D_ATTN and D_ATTN % tn == 0

    @pl.when(n == 0)
    def _():
        h_ref[...] = _modulated_norm(x_ref, shift_ref, scale_ref, g_ref)

    acc = _dot(h_ref[...], w_ref[...])

    def rotary(out_scale):
        cos = cos_ref[...] * out_scale
        sin = sin_ref[...] * out_scale
        for j in range(tn // HEAD_DIM):
            a = acc[:, j * HEAD_DIM:(j + 1) * HEAD_DIM]
            o_ref[:, j * HEAD_DIM:(j + 1) * HEAD_DIM] = (
                a * cos + pltpu.roll(a, HEAD_DIM // 2, axis=1) * sin)

    @pl.when(n < per)
    def _():
        rotary(QK_SCALE)

    @pl.when((n >= per) & (n < 2 * per))
    def _():
        rotary(1.0)

    is_gate = ((n >= 3 * per) & (n < 4 * per)) | (n >= 5 * per)

    @pl.when(is_gate)
    def _():
        o_ref[...] = _silu(acc)

    @pl.when((n >= 2 * per) & jnp.logical_not(is_gate))
    def _():
        o_ref[...] = acc


def _mod_spec(where, bb, width, col_blk):
    layer, _, mod_blk0 = where
    return pl.BlockSpec((None, bb, 1, width),
                        lambda b, i, n: (layer, mod_blk0 + b, 0, col_blk(n)))


def _in_even(x, mods, norm_g, w, cos_rows, sin_rows, where, bb, tm):
    layer, li, _ = where
    bk, t, d = x.shape
    n_out = w.shape[2]
    tn = 512
    rows = bb * tm
    grid = (bk // bb, t // tm, n_out // tn)
    return pl.pallas_call(
        _in_even_kernel,
        grid=grid,
        in_specs=[
            pl.BlockSpec((bb, tm, d), lambda b, i, n: (b, i, 0)),
            _mod_spec(where, bb, d, lambda n: 0),
            _mod_spec(where, bb, d, lambda n: 1),
            pl.BlockSpec((None, 1, d), lambda b, i, n: (layer, 0, 0)),
            pl.BlockSpec((None, d, tn), lambda b, i, n: (li, 0, n)),
            pl.BlockSpec((rows, HEAD_DIM), lambda b, i, n: (i, 0)),
            pl.BlockSpec((rows, HEAD_DIM), lambda b, i, n: (i, 0)),
        ],
        out_specs=pl.BlockSpec((None, rows, tn), lambda b, i, n: (b, i, n)),
        out_shape=jax.ShapeDtypeStruct((bk // bb, (t // tm) * rows, n_out), F32),
        scratch_shapes=[pltpu.VMEM((rows, d), BF16)],
        compiler_params=_params("arbitrary", "arbitrary", "arbitrary"),
        name="in_proj_even",
    )(x, mods, mods, norm_g, w, cos_rows, sin_rows)


def _in_odd_kernel(x_ref, shift_ref, scale_ref, g_ref, wb_ref, wc_ref, wh_ref, wg_ref,
                   cw_ref, pre_ref, o_ref, tail_ref, h_ref, zbuf_ref):
    i = pl.program_id(1)
    n = pl.program_id(2)
    bb, tm, _ = x_ref.shape
    tn = wb_ref.shape[1]
    halo = SUBLANES
    tail_cols = pl.ds(pl.multiple_of(n * tn, tn), tn)

    @pl.when(n == 0)
    def _():
        h_ref[...] = _modulated_norm(x_ref, shift_ref, scale_ref, g_ref)

    h = h_ref[...]
    z = (_dot(h, wc_ref[...]) * _dot(h, wh_ref[...])).reshape(bb, tm, tn)

    @pl.when(i == 0)
    def _():
        zbuf_ref[:, 0:halo, :] = pre_ref[...]

    @pl.when(i > 0)
    def _():
        zbuf_ref[:, 0:halo, :] = tail_ref[:, :, tail_cols]

    zbuf_ref[:, halo:, :] = z
    cw = cw_ref[...]
    yc = zbuf_ref[:, halo - 2:halo - 2 + tm, :] * cw[0:1]
    yc = yc + zbuf_ref[:, halo - 1:halo - 1 + tm, :] * cw[1:2]
    yc = yc + z * cw[2:3]
    tail_ref[:, :, tail_cols] = zbuf_ref[:, tm:tm + halo, :]

    bg = _dot(h, wb_ref[...]).reshape(bb, tm, tn)
    gate = _dot(h, wg_ref[...]).reshape(bb, tm, tn)
    y = bg * yc * _silu(gate)
    o_ref[...] = y.reshape(bb * tm, tn).astype(BF16)


def _in_odd(x, mods, norm_g, w, conv_w, prefix, where, bb, tm):
    layer, li, _ = where
    bk, t, d = x.shape
    tn = 256
    nt = D_CONV // tn
    rows = bb * tm
    grid = (bk // bb, t // tm, nt)
    w_spec = lambda k: pl.BlockSpec((None, d, tn), lambda b, i, n, k=k: (li, 0, k * nt + n))
    return pl.pallas_call(
        _in_odd_kernel,
        grid=grid,
        in_specs=[
            pl.BlockSpec((bb, tm, d), lambda b, i, n: (b, i, 0)),
            _mod_spec(where, bb, d, lambda n: 0),
            _mod_spec(where, bb, d, lambda n: 1),
            pl.BlockSpec((None, 1, d), lambda b, i, n: (layer, 0, 0)),
            w_spec(0), w_spec(1), w_spec(2), w_spec(3),
            pl.BlockSpec((None, CONV_WIDTH, tn), lambda b, i, n: (li, 0, n)),
            pl.BlockSpec((bb, SUBLANES, tn), lambda b, i, n: (b, 0, n)),
        ],
        out_specs=(
            pl.BlockSpec((None, rows, tn), lambda b, i, n: (b, i, n)),
            pl.BlockSpec((bb, SUBLANES, D_CONV), lambda b, i, n: (b, 0, 0)),
        ),
        out_shape=(
            jax.ShapeDtypeStruct((bk // bb, (t // tm) * rows, D_CONV), BF16),
            jax.ShapeDtypeStruct((bk, SUBLANES, D_CONV), F32),
        ),
        scratch_shapes=[
            pltpu.VMEM((rows, d), BF16),
            pltpu.VMEM((bb, SUBLANES + tm, tn), F32),
        ],
        compiler_params=_params("arbitrary", "arbitrary", "arbitrary"),
        name="in_proj_odd_conv",
    )(x, mods, mods, norm_g, w, w, w, w, conv_w, prefix)


def _out_proj_kernel(a1_ref, a2_ref, w1_ref, w2_ref, x_ref, gate_ref, o_ref):
    bb, tm, tn = x_ref.shape
    acc = _dot(a1_ref[...].astype(BF16), w1_ref[...])
    acc = acc + _dot(a2_ref[...].astype(BF16), w2_ref[...])
    o_ref[...] = x_ref[...] + gate_ref[...] * acc.reshape(bb, tm, tn)


def _out_proj(a1, a1_blk, a2, a2_blk, w, x, mods, where, bb, tm):
    _, li, _ = where
    bk, t, d = x.shape
    kh = w.shape[1] // 2
    tn = 512
    rows = bb * tm
    gate_blk0 = 2 * d // tn
    return pl.pallas_call(
        _out_proj_kernel,
        grid=(bk // bb, t // tm, d // tn),
        in_specs=[
            pl.BlockSpec((None, rows, kh), lambda b, i, n: (b, i, a1_blk)),
            pl.BlockSpec((None, rows, kh), lambda b, i, n: (b, i, a2_blk)),
            pl.BlockSpec((None, kh, tn), lambda b, i, n: (li, 0, n)),
            pl.BlockSpec((None, kh, tn), lambda b, i, n: (li, 1, n)),
            pl.BlockSpec((bb, tm, tn), lambda b, i, n: (b, i, n)),
            _mod_spec(where, bb, tn, lambda n: gate_blk0 + n),
        ],
        out_specs=pl.BlockSpec((bb, tm, tn), lambda b, i, n: (b, i, n)),
        out_shape=jax.ShapeDtypeStruct((bk, t, d), F32),
        compiler_params=_params("arbitrary", "arbitrary", "arbitrary"),
        name="out_proj_residual",
    )(a1, a2, w, w, x, mods)


def _attn_prompt_kernel(q_ref, k_ref, v_ref, ga_ref, bias_ref, o_ref, kt_ref, vt_ref,
                        acc_ref, m_ref, l_ref):
    t = q_ref.shape[0]
    bq = KEYS_PER_PATTERN
    dilations = sorted((d for _, d in PATTERNS), reverse=True)
    assert dilations[-1] == 1

    keep, nh, _ = kt_ref.shape
    head = pl.program_id(1)
    for src_ref, dst_ref in ((k_ref, kt_ref), (v_ref, vt_ref)):
        dst = dst_ref.reshape(keep * nh, HEAD_DIM)
        dst[pl.ds(head, keep, stride=nh), :] = src_ref[pl.ds(t - keep, keep), :]

    for pi, d in enumerate(dilations):
        span = bq * d
        nblk = t // span
        first = pi == 0
        last = pi == len(dilations) - 1

        def rows(ref, start, n, d=d):
            if d == 1:
                return ref[pl.ds(pl.multiple_of(start, bq), n), :]
            return ref[pl.ds(start, n, stride=d), :]

        def put(ref, start, val, d=d):
            if d == 1:
                ref[pl.ds(pl.multiple_of(start, bq), bq), :] = val
            else:
                ref[pl.ds(start, bq, stride=d), :] = val

        def body(it, carry, nblk=nblk, span=span, rows=rows, put=put, first=first, last=last):
            bases, kbases, has_prevs = [], [], []
            for u in range(ATTN_UNROLL):
                itu = it * ATTN_UNROLL + u
                blk = itu % nblk
                base = itu // nblk + blk * span
                has_prev = jnp.minimum(blk, 1)
                bases.append(base)
                kbases.append(base - has_prev * span)
                has_prevs.append(has_prev)
            stack = lambda ref, starts, n: jnp.stack([rows(ref, s, n) for s in starts])

            def keys(ref):
                if nblk == 2 and ATTN_UNROLL % 2 == 0:
                    pairs = [rows(ref, s, 2 * bq).astype(BF16) for s in kbases[::2]]
                    return jnp.stack([pairs[u // 2] for u in range(ATTN_UNROLL)])
                return stack(ref, kbases, 2 * bq).astype(BF16)

            q = stack(q_ref, bases, bq).astype(BF16)
            k = keys(k_ref)
            v = keys(v_ref)
            s = jnp.einsum('uqd,ukd->uqk', q, k, preferred_element_type=F32)
            s = s + jnp.stack([bias_ref[hp] for hp in has_prevs])
            m = jnp.max(jnp.maximum(s[:, :, :bq], s[:, :, bq:]), axis=-1, keepdims=True)
            p = jnp.exp2(s - m)
            l = jnp.sum(p[:, :, :bq] + p[:, :, bq:], axis=-1, keepdims=True)
            num = jnp.einsum('uqk,ukd->uqd', p.astype(BF16), v, preferred_element_type=F32)
            m = jnp.broadcast_to(m, num.shape)
            l = jnp.broadcast_to(l, num.shape)
            if not first:
                m_old = stack(m_ref, bases, bq)
                m_new = jnp.maximum(m_old, m)
                a_old = jnp.exp2(m_old - m_new)
                a_new = jnp.exp2(m - m_new)
                num = a_old * stack(acc_ref, bases, bq) + a_new * num
                l = a_old * stack(l_ref, bases, bq) + a_new * l
                m = m_new
            for u, base in enumerate(bases):
                if last:
                    rows_out = pl.ds(pl.multiple_of(base, bq), bq)
                    o_ref[rows_out, :] = (num[u] / l[u] * ga_ref[rows_out, :]).astype(BF16)
                else:
                    put(acc_ref, base, num[u])
                    put(m_ref, base, m[u])
                    put(l_ref, base, l[u])
            return carry

        lax.fori_loop(0, t // (bq * ATTN_UNROLL), body, 0)


def _attn_prompt(proj):
    b, t, _ = proj.shape
    for w, d in PATTERNS:
        assert w // d == KEYS_PER_PATTERN and t % (2 * KEYS_PER_PATTERN * d) == 0
    assert (t // KEYS_PER_PATTERN) % ATTN_UNROLL == 0
    col = lambda off: pl.BlockSpec((None, t, HEAD_DIM), lambda bi, h, off=off: (bi, 0, off + h))
    bq = KEYS_PER_PATTERN
    qi = jnp.arange(bq, dtype=jnp.int32)[:, None]
    kj = jnp.arange(2 * bq, dtype=jnp.int32)[None, :]
    visible = jnp.stack([kj <= qi, (kj >= qi) & (kj <= qi + bq)])
    bias = jnp.where(visible, 0.0, NEG_INF).astype(F32)
    keep = min(WINDOW_MAX, t)
    tail = pl.BlockSpec((None, keep, N_HEADS, HEAD_DIM), lambda bi, h: (bi, 0, 0, 0),
                        pipeline_mode=pl.Buffered(1))
    tail_shape = jax.ShapeDtypeStruct((b, keep, N_HEADS, HEAD_DIM), F32)
    return pl.pallas_call(
        _attn_prompt_kernel,
        grid=(b, N_HEADS),
        in_specs=[col(0), col(N_HEADS), col(2 * N_HEADS), col(3 * N_HEADS),
                  pl.BlockSpec((2, bq, 2 * bq), lambda bi, h: (0, 0, 0))],
        out_specs=(pl.BlockSpec((None, t, HEAD_DIM), lambda bi, h: (bi, 0, h)), tail, tail),
        out_shape=(jax.ShapeDtypeStruct((b, t, D_ATTN), BF16), tail_shape, tail_shape),
        scratch_shapes=[pltpu.VMEM((t, HEAD_DIM), F32)] * 3,
        compiler_params=_params("arbitrary", "arbitrary"),
        name="attn_prompt",
    )(proj, proj, proj, proj, bias)


def _pattern_count(delta):
    cnt = jnp.zeros(delta.shape, F32)
    for w, d in PATTERNS:
        assert d & (d - 1) == 0
        ok = (delta >= 0) & (delta <= w) & (jnp.bitwise_and(delta, d - 1) == 0)
        cnt = cnt + ok.astype(F32)
    return cnt


def _attn_sample_kernel(q_ref, kn_ref, vn_ref, ga_ref, ka_ref, kb_ref, va_ref, vb_ref, o_ref,
                        *, wb):
    tq = q_ref.shape[0]
    nh = N_HEADS
    assert tq & (tq - 1) == 0 and nh & (nh - 1) == 0
    tq_bits, nh_bits, sub_bits = (n.bit_length() - 1 for n in (tq, nh, SUBLANES))
    nq = nh * tq
    scale = HEAD_DIM ** -0.5

    def heads_to_rows(ref):
        return jnp.concatenate(
            [ref[:, h * HEAD_DIM:(h + 1) * HEAD_DIM] for h in range(nh)], axis=0)

    def flat(ref):
        g, j, h, d = ref.shape
        return ref[...].reshape(g * j * h, d).astype(BF16)

    def window_weights(n_groups, group0, residue0):
        shape = (nq, n_groups * SUBLANES * nh)
        r = lax.broadcasted_iota(jnp.int32, shape, 0)
        c = lax.broadcasted_iota(jnp.int32, shape, 1)
        key_head = c & (nh - 1)
        residue = residue0 + ((c >> nh_bits) & (SUBLANES - 1))
        key_row = (group0 + (c >> (nh_bits + sub_bits))) * SAMPLE_GROUP + residue
        delta = wb + (r & (tq - 1)) - key_row
        return jnp.where((r >> tq_bits) == key_head, _pattern_count(delta), 0.0)

    r = lax.broadcasted_iota(jnp.int32, (nq, nq), 0)
    c = lax.broadcasted_iota(jnp.int32, (nq, nq), 1)
    w_new = jnp.where((r >> tq_bits) == (c >> tq_bits),
                      _pattern_count((r & (tq - 1)) - (c & (tq - 1))), 0.0)
    n_dense = kb_ref.shape[0]
    parts = (
        (flat(ka_ref), flat(va_ref), window_weights(ka_ref.shape[0], 0, 0)),
        (flat(kb_ref), flat(vb_ref),
         window_weights(n_dense, wb // SAMPLE_GROUP - n_dense, SUBLANES)),
        (heads_to_rows(kn_ref).astype(BF16), heads_to_rows(vn_ref).astype(BF16), w_new),
    )
    q = heads_to_rows(q_ref).astype(BF16)
    scores = [jnp.where(wgt > 0, _dot_nt(q, k), NEG_INF) for k, _, wgt in parts]
    m = functools.reduce(jnp.maximum, [jnp.max(s, axis=-1, keepdims=True) for s in scores])
    den = jnp.zeros((nq, 1), F32)
    num = jnp.zeros((nq, HEAD_DIM), F32)
    for s, (_, v, wgt) in zip(scores, parts):
        p = wgt * jnp.exp2(s - m)
        den = den + jnp.sum(p, axis=-1, keepdims=True)
        num = num + _dot(p.astype(BF16), v)
    out = num / den
    for h in range(nh):
        cs = slice(h * HEAD_DIM, (h + 1) * HEAD_DIM)
        o_ref[:, cs] = out[h * tq:(h + 1) * tq] * ga_ref[:, cs]


def _attn_sample(proj, win_k, win_v, li, n_seq, tq):
    wb = win_k.shape[2]
    dense_reach = max(w for w, d in PATTERNS if d < SAMPLE_GROUP)
    assert max(d for _, d in PATTERNS) == SAMPLE_GROUP == 2 * SUBLANES and tq == SUBLANES
    assert wb % SAMPLE_GROUP == 0 and dense_reach % SAMPLE_GROUP == 0 and wb % dense_reach == 0
    n_groups = wb // SAMPLE_GROUP
    n_dense = dense_reach // SAMPLE_GROUP
    win_k = win_k.reshape(-1, n_seq, n_groups, SAMPLE_GROUP, N_HEADS, HEAD_DIM)
    win_v = win_v.reshape(-1, n_seq, n_groups, SAMPLE_GROUP, N_HEADS, HEAD_DIM)
    col = lambda blk: pl.BlockSpec((None, tq, D_ATTN), lambda s, blk=blk: (0, s, blk))
    sparse = pl.BlockSpec((None, None, n_groups, SUBLANES, N_HEADS, HEAD_DIM),
                          lambda s: (li, s, 0, 0, 0, 0))
    dense = pl.BlockSpec((None, None, n_dense, SUBLANES, N_HEADS, HEAD_DIM),
                         lambda s: (li, s, n_groups // n_dense - 1, 1, 0, 0))
    return pl.pallas_call(
        functools.partial(_attn_sample_kernel, wb=wb),
        grid=(n_seq,),
        in_specs=[col(0), col(1), col(2), col(3), sparse, dense, sparse, dense],
        out_specs=pl.BlockSpec((None, tq, D_ATTN), lambda s: (0, s, 0)),
        out_shape=jax.ShapeDtypeStruct((1, n_seq * tq, D_ATTN), F32),
        compiler_params=_params("arbitrary"),
        name="attn_sample",
    )(proj, proj, proj, proj, win_k, win_k, win_v, win_v)


def _pool_kernel(u_ref, halo_ref, pre_ref, gp_ref, w_ref, s_ref, o_ref, ext_ref, *, pos0):
    i = pl.program_id(1)
    g = pl.program_id(2)
    tm = u_ref.shape[0]
    npre = POOL_MAX

    @pl.when(i == 0)
    def _():
        ext_ref[0:npre, :] = pre_ref[...]

    @pl.when(i > 0)
    def _():
        ext_ref[0:npre, :] = halo_ref[...]

    u = u_ref[...]
    ext_ref[npre:, :] = u
    pos = pos0 + i * tm + lax.broadcasted_iota(jnp.int32, (tm, 1), 0)

    for gi, w in enumerate(POOL_WINDOWS):
        @pl.when(g == gi)
        def _(w=w):
            win = u
            for j in range(1, w):
                win = win + ext_ref[npre - j:npre - j + tm, :]
            inv_cnt = 1.0 / jnp.minimum(w, pos + 1).astype(F32)
            pooled = win * inv_cnt - u
            mixed = _dot(pooled.astype(BF16), w_ref[...].astype(BF16)) * s_ref[...]
            o_ref[...] = (mixed * gp_ref[...]).astype(o_ref.dtype)


def _pool(proj, prefix, pool_w, pool_scale, li, seq_rows, tm, pos0, out_dtype):
    g_, r_, _ = proj.shape
    n_seq = g_ * (r_ // seq_rows)
    nt = seq_rows // tm
    per_g = r_ // seq_rows
    ng = len(POOL_WINDOWS)
    u_blk0 = 4 * D_ATTN // POOL_GROUP
    gp_blk0 = u_blk0 + ng
    hb = tm // POOL_MAX

    def seq_map(blk0):
        return lambda s, i, g: (s // per_g, (s % per_g) * nt + i, blk0 + g)

    def halo_map(s, i, g):
        return (s // per_g, jnp.maximum(((s % per_g) * nt + i) * hb - 1, 0), u_blk0 + g)

    return pl.pallas_call(
        functools.partial(_pool_kernel, pos0=pos0),
        grid=(n_seq, nt, ng),
        in_specs=[
            pl.BlockSpec((None, tm, POOL_GROUP), seq_map(u_blk0)),
            pl.BlockSpec((None, POOL_MAX, POOL_GROUP), halo_map),
            pl.BlockSpec((None, POOL_MAX, POOL_GROUP), lambda s, i, g: (s, 0, g)),
            pl.BlockSpec((None, tm, POOL_GROUP), seq_map(gp_blk0)),
            pl.BlockSpec((None, None, POOL_GROUP, POOL_GROUP), lambda s, i, g: (li, g, 0, 0)),
            pl.BlockSpec((None, 1, POOL_GROUP), lambda s, i, g: (li, 0, g)),
        ],
        out_specs=pl.BlockSpec((None, tm, POOL_GROUP), seq_map(0)),
        out_shape=jax.ShapeDtypeStruct((g_, r_, D_POOL), out_dtype),
        scratch_shapes=[pltpu.VMEM((POOL_MAX + tm, POOL_GROUP), F32)],
        compiler_params=_params("arbitrary", "arbitrary", "arbitrary"),
        name="pool_mixer",
    )(proj, proj, prefix, proj, pool_w, pool_scale)


def _final_norm_kernel(x_ref, g_ref, o_ref):
    x = x_ref[...]
    ms = jnp.mean(x * x, axis=-1, keepdims=True)
    o_ref[...] = x * lax.rsqrt(ms + RMS_EPS) * g_ref[...]


def _final_norm(x, g, bb, tm):
    bk, t, d = x.shape
    return pl.pallas_call(
        _final_norm_kernel,
        grid=(bk // bb, t // tm),
        in_specs=[pl.BlockSpec((bb, tm, d), lambda b, i: (b, i, 0)),
                  pl.BlockSpec((1, d), lambda b, i: (0, 0))],
        out_specs=pl.BlockSpec((bb, tm, d), lambda b, i: (b, i, 0)),
        out_shape=jax.ShapeDtypeStruct((bk, t, d), F32),
        compiler_params=_params("arbitrary", "arbitrary"),
        name="final_norm",
    )(x, g)


def _trunk(x, mods, mod_row0, pos0, state, weights, bb, tm):
    (norm_g, w_in_even, w_out_even, pool_w, pool_scale, w_in_odd, conv_w, w_out_odd,
     final_g) = weights
    bk, t, _ = x.shape
    fresh = state is None
    assert mod_row0 % bb == 0
    cos, sin = _rope_tables(t, pos0)
    if bb > 1:
        cos, sin = jnp.tile(cos, (bb, 1)), jnp.tile(sin, (bb, 1))
    new_k, new_v, new_pool, new_conv = [], [], [], []
    for layer in range(DEPTH):
        li = layer // 2
        where = (layer, li, mod_row0 // bb)
        if layer % 2 == 0:
            proj = _in_even(x, mods, norm_g, w_in_even, cos, sin, where, bb, tm)
            seq = proj.reshape(bk, t, D_IN_EVEN)
            u = seq[:, :, 4 * D_ATTN:4 * D_ATTN + D_POOL]
            if fresh:
                pool_prefix = jnp.zeros((bk, POOL_MAX - 1, D_POOL), F32)
                attn, k, v = _attn_prompt(proj)
                new_pool.append(u[:, -(POOL_MAX - 1):])
            else:
                win_k, win_v, pool_st, _ = state
                pool_prefix = pool_st[li]
                attn = _attn_sample(proj, win_k, win_v, li, bk, t)
                k = seq[:, :, D_ATTN:2 * D_ATTN].reshape(bk, t, N_HEADS, HEAD_DIM)
                v = seq[:, :, 2 * D_ATTN:3 * D_ATTN].reshape(bk, t, N_HEADS, HEAD_DIM)
                new_pool.append(
                    jnp.concatenate([pool_prefix, u], axis=1)[:, -(POOL_MAX - 1):])
            new_k.append(k)
            new_v.append(v)
            prefix = jnp.pad(pool_prefix, ((0, 0), (1, 0), (0, 0)))
            pooled = _pool(proj, prefix, pool_w, pool_scale, li, t,
                           min(t, 1024), pos0, BF16 if fresh else F32)
            x = _out_proj(attn, 0, pooled, 0, w_out_even, x, mods, where, bb, tm)
        else:
            if fresh:
                prefix = jnp.zeros((bk, SUBLANES, D_CONV), F32)
            else:
                prefix = jnp.pad(state[3][li], ((0, 0), (SUBLANES - (CONV_WIDTH - 1), 0), (0, 0)))
            y, tail = _in_odd(x, mods, norm_g, w_in_odd, conv_w, prefix, where, bb, tm)
            new_conv.append(tail[:, SUBLANES - (CONV_WIDTH - 1):])
            x = _out_proj(y, 0, y, 1, w_out_odd, x, mods, where, bb, tm)
    y = _final_norm(x, final_g[None, :], bb, tm)
    return y, jnp.stack(new_k), jnp.stack(new_v), jnp.stack(new_pool), jnp.stack(new_conv)


def kernel(x_prompt, x_sample, c_prompt, c_sample, state_win_k, state_win_v, state_pool,
           state_conv, ada_w, ada_b, norm_g, w_in_even, w_out_even, pool_w, pool_scale,
           w_in_odd, conv_w, w_out_odd, final_g):
    nb, t_p, _ = x_prompt.shape
    ns, t_s, _ = x_sample.shape
    n_c = nb + ns
    c_rows = -(-n_c // SUBLANES) * SUBLANES
    c_all = jnp.pad(jnp.concatenate([c_sample, c_prompt], axis=0), ((0, c_rows - n_c), (0, 0)))
    mods = _ada_modulation(c_all, ada_w, ada_b)[:, :, None, :]
    weights = (norm_g[:, None, :], w_in_even.astype(BF16), w_out_even.astype(BF16), pool_w,
               pool_scale[:, None, :], w_in_odd.astype(BF16), conv_w, w_out_odd.astype(BF16),
               final_g)

    y_p, kp, vp, poolp, convp = _trunk(x_prompt, mods, ns, 0, None, weights, 1, 1024)

    state = (state_win_k, state_win_v, state_pool, state_conv)
    y_s, ksm, vsm, pools, convs = _trunk(x_sample, mods, 0, PAST_LEN, state, weights, ns, t_s)
    return (y_p, y_s, kp, vp, poolp, convp, ksm, vsm, pools, convs)
```

```python
import functools
import math

import jax
import jax.numpy as jnp
from jax import lax
from jax.experimental import pallas as pl
from jax.experimental.pallas import tpu as pltpu

F32 = jnp.float32
BF16 = jnp.bfloat16

D_MODEL = 2048
DEPTH = 4
PAST_LEN = 16384
D_ATTN = D_MODEL // 2
HEAD_DIM = 128
N_HEADS = D_ATTN // HEAD_DIM
PATTERNS = ((128, 1), (512, 4), (2048, 16))
KEYS_PER_PATTERN = 128
ATTN_UNROLL = 8
SAMPLE_GROUP = max(d for _, d in PATTERNS)
MATMUL_ROW_CHUNK = 128
WINDOW_MAX = max(w for w, _ in PATTERNS)
ROPE_THETA = 10000.0
D_POOL = D_MODEL // 2
POOL_WINDOWS = (2, 4, 8, 16)
POOL_GROUP = D_POOL // len(POOL_WINDOWS)
POOL_MAX = max(POOL_WINDOWS)
D_CONV = D_MODEL
CONV_WIDTH = 3
D_IN_EVEN = 4 * D_ATTN + 2 * D_POOL
RMS_EPS = 1e-6
NEG_INF = -1e30
QK_SCALE = HEAD_DIM ** -0.5 * math.log2(math.e)

LANES = 128
SUBLANES = 8
VMEM_LIMIT = 48 * 1024 * 1024


def _params(*semantics):
    return pltpu.CompilerParams(dimension_semantics=semantics,
                                vmem_limit_bytes=VMEM_LIMIT)


def _silu(v):
    return v / (1.0 + jnp.exp(-v))


def _dot(a, b):
    return jnp.dot(a, b, preferred_element_type=F32)


def _dot_nt(a, b):
    return lax.dot_general(a, b, (((1,), (1,)), ((), ())), preferred_element_type=F32)


def _rope_kernel(inv_ref, cos_ref, sin_ref, *, pos0):
    t = cos_ref.shape[0]
    pos = (lax.broadcasted_iota(jnp.int32, (t, HEAD_DIM), 0) + pos0).astype(F32)
    lane = lax.broadcasted_iota(jnp.int32, (t, HEAD_DIM), 1)
    ang = pos * inv_ref[...]
    s = jnp.sin(ang)
    cos_ref[...] = jnp.cos(ang)
    sin_ref[...] = jnp.where(lane < HEAD_DIM // 2, -s, s)


def _rope_tables(t, pos0):
    half = HEAD_DIM // 2
    inv = jnp.power(ROPE_THETA, -jnp.arange(half, dtype=F32) / half)
    inv = jnp.concatenate([inv, inv])[None, :]
    return pl.pallas_call(
        functools.partial(_rope_kernel, pos0=pos0),
        out_shape=(jax.ShapeDtypeStruct((t, HEAD_DIM), F32),
                   jax.ShapeDtypeStruct((t, HEAD_DIM), F32)),
        name="rope_tables",
    )(inv)


def _ada_kernel(c_ref, w_ref, b_ref, o_ref):
    a = _silu(c_ref[...]).astype(BF16)
    o_ref[...] = _dot(a, w_ref[...].astype(BF16)) + b_ref[...]


def _ada_modulation(c_all, ada_w, ada_b):
    rows = c_all.shape[0]
    n_out = ada_w.shape[2]
    tn = 1024
    return pl.pallas_call(
        _ada_kernel,
        grid=(DEPTH, n_out // tn),
        in_specs=[
            pl.BlockSpec((rows, D_MODEL), lambda l, n: (0, 0)),
            pl.BlockSpec((None, D_MODEL, tn), lambda l, n: (l, 0, n)),
            pl.BlockSpec((None, 1, tn), lambda l, n: (l, 0, n)),
        ],
        out_specs=pl.BlockSpec((None, rows, tn), lambda l, n: (l, 0, n)),
        out_shape=jax.ShapeDtypeStruct((DEPTH, rows, n_out), F32),
        compiler_params=_params("arbitrary", "arbitrary"),
        name="ada_modulation",
    )(c_all, ada_w, ada_b.reshape(DEPTH, 1, n_out))


def _row_chunks(bb, tm):
    chunk = min(tm, MATMUL_ROW_CHUNK) if bb == 1 else bb * tm
    return [slice(c * chunk, (c + 1) * chunk) for c in range(bb * tm // chunk)]


def _normalized_rows(first, rs, bb, x_ref, shift_ref, scale_ref, g_ref, h_ref):
    if not first:
        return h_ref[rs, :]
    h = _modulated_norm(x_ref, shift_ref, scale_ref, g_ref, rs if bb == 1 else None)
    h_ref[rs, :] = h
    return h


def _modulated_norm(x_ref, shift_ref, scale_ref, g_ref, rows=None):
    x = x_ref[...] if rows is None else x_ref[:, rows, :]
    ms = jnp.mean(x * x, axis=-1, keepdims=True)
    y = x * lax.rsqrt(ms + RMS_EPS) * g_ref[...]
    h = y * (1.0 + scale_ref[...]) + shift_ref[...]
    bb, tm, d = h.shape
    return h.reshape(bb * tm, d).astype(BF16)


def _in_even_kernel(x_ref, shift_ref, scale_ref, g_ref, w_ref, cos_ref, sin_ref,
                    o_ref, h_ref):
    n = pl.program_id(2)
    tn = w_ref.shape[1]
    per = D_ATTN // tn
    assert D_POOL == D_ATTN and D_ATTN % tn == 0

    is_gate = ((n >= 3 * per) & (n < 4 * per)) | (n >= 5 * per)
    bb, tm, _ = x_ref.shape

    def step(first):
        for rs in _row_chunks(bb, tm):
            h = _normalized_rows(first, rs, bb, x_ref, shift_ref, scale_ref, g_ref, h_ref)
            acc = _dot(h, w_ref[...])
            cos = cos_ref[rs, :]
            sin = sin_ref[rs, :]
            for j in range(tn // HEAD_DIM):
                cs = slice(j * HEAD_DIM, (j + 1) * HEAD_DIM)
                a = acc[:, cs]
                lin = a * cos + pltpu.roll(a, HEAD_DIM // 2, axis=1) * sin
                o_ref[rs, cs] = jnp.where(is_gate, _silu(lin), lin)

    pl.when(n == 0)(functools.partial(step, True))
    pl.when(n > 0)(functools.partial(step, False))


def _mod_spec(where, bb, width, col_blk):
    layer, _, mod_blk0 = where
    return pl.BlockSpec((None, bb, 1, width),
                        lambda b, i, n: (layer, mod_blk0 + b, 0, col_blk(n)))


def _rotation_coefficients(cos_rows, sin_rows):
    ones, zeros = jnp.ones_like(cos_rows), jnp.zeros_like(sin_rows)
    return jnp.stack([jnp.stack([cos_rows * QK_SCALE, sin_rows * QK_SCALE]),
                      jnp.stack([cos_rows, sin_rows]),
                      jnp.stack([ones, zeros])])


def _in_even(x, mods, norm_g, w, coef, where, bb, tm):
    layer, li, _ = where
    bk, t, d = x.shape
    n_out = w.shape[2]
    tn = 512
    per = D_ATTN // tn
    rows = bb * tm
    grid = (bk // bb, t // tm, n_out // tn)
    coef_spec = lambda which: pl.BlockSpec(
        (None, None, rows, HEAD_DIM),
        lambda b, i, n: (jnp.minimum(n // per, 2), which, i, 0))
    return pl.pallas_call(
        _in_even_kernel,
        grid=grid,
        in_specs=[
            pl.BlockSpec((bb, tm, d), lambda b, i, n: (b, i, 0)),
            _mod_spec(where, bb, d, lambda n: 0),
            _mod_spec(where, bb, d, lambda n: 1),
            pl.BlockSpec((None, 1, d), lambda b, i, n: (layer, 0, 0)),
            pl.BlockSpec((None, d, tn), lambda b, i, n: (li, 0, n)),
            coef_spec(0),
            coef_spec(1),
        ],
        out_specs=pl.BlockSpec((None, rows, tn), lambda b, i, n: (b, i, n)),
        out_shape=jax.ShapeDtypeStruct((bk // bb, (t // tm) * rows, n_out), F32),
        scratch_shapes=[pltpu.VMEM((rows, d), BF16)],
        compiler_params=_params("arbitrary", "arbitrary", "arbitrary"),
        name="in_proj_even",
    )(x, mods, mods, norm_g, w, coef, coef)


def _in_odd_kernel(x_ref, shift_ref, scale_ref, g_ref, wb_ref, wc_ref, wh_ref, wg_ref,
                   cw_ref, pre_ref, o_ref, tail_ref, h_ref, zbuf_ref):
    i = pl.program_id(1)
    n = pl.program_id(2)
    bb, tm, _ = x_ref.shape
    tn = wb_ref.shape[1]
    halo = SUBLANES
    tail_cols = pl.ds(pl.multiple_of(n * tn, tn), tn)

    @pl.when((n == 0) & (i == 0))
    def _():
        tail_ref[...] = pre_ref[...]

    def step(first):
        zbuf_ref[:, 0:halo, :] = tail_ref[:, :, tail_cols]
        cw = cw_ref[...]
        for rs in _row_chunks(bb, tm):
            lo, hi = (rs.start, rs.stop) if bb == 1 else (0, tm)
            h = _normalized_rows(first, rs, bb, x_ref, shift_ref, scale_ref, g_ref, h_ref)
            z = (_dot(h, wc_ref[...]) * _dot(h, wh_ref[...])).reshape(bb, hi - lo, tn)
            zbuf_ref[:, halo + lo:halo + hi, :] = z
            yc = zbuf_ref[:, halo - 2 + lo:halo - 2 + hi, :] * cw[0:1]
            yc = yc + zbuf_ref[:, halo - 1 + lo:halo - 1 + hi, :] * cw[1:2]
            yc = yc + z * cw[2:3]
            bg = _dot(h, wb_ref[...]).reshape(bb, hi - lo, tn)
            gate = _dot(h, wg_ref[...]).reshape(bb, hi - lo, tn)
            y = bg * yc * _silu(gate)
            o_ref[rs, :] = y.reshape(rs.stop - rs.start, tn).astype(BF16)
        tail_ref[:, :, tail_cols] = zbuf_ref[:, tm:tm + halo, :]

    pl.when(n == 0)(functools.partial(step, True))
    pl.when(n > 0)(functools.partial(step, False))


def _in_odd(x, mods, norm_g, w, conv_w, prefix, where, bb, tm):
    layer, li, _ = where
    bk, t, d = x.shape
    tn = 256
    nt = D_CONV // tn
    rows = bb * tm
    grid = (bk // bb, t // tm, nt)
    w_spec = lambda k: pl.BlockSpec((None, d, tn), lambda b, i, n, k=k: (li, 0, k * nt + n))
    return pl.pallas_call(
        _in_odd_kernel,
        grid=grid,
        in_specs=[
            pl.BlockSpec((bb, tm, d), lambda b, i, n: (b, i, 0)),
            _mod_spec(where, bb, d, lambda n: 0),
            _mod_spec(where, bb, d, lambda n: 1),
            pl.BlockSpec((None, 1, d), lambda b, i, n: (layer, 0, 0)),
            w_spec(0), w_spec(1), w_spec(2), w_spec(3),
            pl.BlockSpec((None, CONV_WIDTH, tn), lambda b, i, n: (li, 0, n)),
            pl.BlockSpec((bb, SUBLANES, D_CONV), lambda b, i, n: (b, 0, 0)),
        ],
        out_specs=(
            pl.BlockSpec((None, rows, tn), lambda b, i, n: (b, i, n)),
            pl.BlockSpec((bb, SUBLANES, D_CONV), lambda b, i, n: (b, 0, 0)),
        ),
        out_shape=(
            jax.ShapeDtypeStruct((bk // bb, (t // tm) * rows, D_CONV), BF16),
            jax.ShapeDtypeStruct((bk, SUBLANES, D_CONV), F32),
        ),
        scratch_shapes=[
            pltpu.VMEM((rows, d), BF16),
            pltpu.VMEM((bb, SUBLANES + tm, tn), F32),
        ],
        compiler_params=_params("arbitrary", "arbitrary", "arbitrary"),
        name="in_proj_odd_conv",
    )(x, mods, mods, norm_g, w, w, w, w, conv_w, prefix)


def _out_proj_kernel(a1_ref, a2_ref, w1_ref, w2_ref, x_ref, gate_ref, o_ref):
    bb, tm, tn = x_ref.shape
    acc = _dot(a1_ref[...].astype(BF16), w1_ref[...])
    acc = acc + _dot(a2_ref[...].astype(BF16), w2_ref[...])
    o_ref[...] = x_ref[...] + gate_ref[...] * acc.reshape(bb, tm, tn)


def _out_proj(a1, a1_blk, a2, a2_blk, w, x, mods, where, bb, tm):
    _, li, _ = where
    bk, t, d = x.shape
    kh = w.shape[1] // 2
    tn = 512
    rows = bb * tm
    gate_blk0 = 2 * d // tn
    return pl.pallas_call(
        _out_proj_kernel,
        grid=(bk // bb, t // tm, d // tn),
        in_specs=[
            pl.BlockSpec((None, rows, kh), lambda b, i, n: (b, i, a1_blk)),
            pl.BlockSpec((None, rows, kh), lambda b, i, n: (b, i, a2_blk)),
            pl.BlockSpec((None, kh, tn), lambda b, i, n: (li, 0, n)),
            pl.BlockSpec((None, kh, tn), lambda b, i, n: (li, 1, n)),
            pl.BlockSpec((bb, tm, tn), lambda b, i, n: (b, i, n)),
            _mod_spec(where, bb, tn, lambda n: gate_blk0 + n),
        ],
        out_specs=pl.BlockSpec((bb, tm, tn), lambda b, i, n: (b, i, n)),
        out_shape=jax.ShapeDtypeStruct((bk, t, d), F32),
        compiler_params=_params("arbitrary", "arbitrary", "arbitrary"),
        name="out_proj_residual",
    )(a1, a2, w, w, x, mods)


def _attn_prompt_kernel(q_ref, k_ref, v_ref, ga_ref, bias_ref, o_ref, kt_ref, vt_ref,
                        acc_ref, m_ref, l_ref):
    t = q_ref.shape[0]
    bq = KEYS_PER_PATTERN
    dilations = sorted((d for _, d in PATTERNS), reverse=True)
    assert dilations[-1] == 1

    keep, nh, _ = kt_ref.shape
    head = pl.program_id(1)
    for src_ref, dst_ref in ((k_ref, kt_ref), (v_ref, vt_ref)):
        dst = dst_ref.reshape(keep * nh, HEAD_DIM)
        dst[pl.ds(head, keep, stride=nh), :] = src_ref[pl.ds(t - keep, keep), :]

    for pi, d in enumerate(dilations):
        span = bq * d
        nblk = t // span
        first = pi == 0
        last = pi == len(dilations) - 1

        def rows(ref, start, n, d=d):
            if d == 1:
                return ref[pl.ds(pl.multiple_of(start, bq), n), :]
            return ref[pl.ds(start, n, stride=d), :]

        def put(ref, start, val, d=d):
            if d == 1:
                ref[pl.ds(pl.multiple_of(start, bq), bq), :] = val
            else:
                ref[pl.ds(start, bq, stride=d), :] = val

        def body(it, carry, nblk=nblk, span=span, rows=rows, put=put, first=first, last=last):
            bases, kbases, has_prevs = [], [], []
            for u in range(ATTN_UNROLL):
                itu = it * ATTN_UNROLL + u
                blk = itu % nblk
                base = itu // nblk + blk * span
                has_prev = jnp.minimum(blk, 1)
                bases.append(base)
                kbases.append(base - has_prev * span)
                has_prevs.append(has_prev)
            stack = lambda ref, starts, n: jnp.stack([rows(ref, s, n) for s in starts])

            def keys(ref):
                if nblk == 2 and ATTN_UNROLL % 2 == 0:
                    pairs = [rows(ref, s, 2 * bq).astype(BF16) for s in kbases[::2]]
                    return jnp.stack([pairs[u // 2] for u in range(ATTN_UNROLL)])
                return stack(ref, kbases, 2 * bq).astype(BF16)

            q = stack(q_ref, bases, bq).astype(BF16)
            k = keys(k_ref)
            v = keys(v_ref)
            s = jnp.einsum('uqd,ukd->uqk', q, k, preferred_element_type=F32)
            s = s + jnp.stack([bias_ref[hp] for hp in has_prevs])
            m = jnp.max(jnp.maximum(s[:, :, :bq], s[:, :, bq:]), axis=-1, keepdims=True)
            p = jnp.exp2(s - m)
            l = jnp.sum(p[:, :, :bq] + p[:, :, bq:], axis=-1, keepdims=True)
            num = jnp.einsum('uqk,ukd->uqd', p.astype(BF16), v, preferred_element_type=F32)
            m = jnp.broadcast_to(m, num.shape)
            l = jnp.broadcast_to(l, num.shape)
            if not first:
                m_old = stack(m_ref, bases, bq)
                m_new = jnp.maximum(m_old, m)
                a_old = jnp.exp2(m_old - m_new)
                a_new = jnp.exp2(m - m_new)
                num = a_old * stack(acc_ref, bases, bq) + a_new * num
                l = a_old * stack(l_ref, bases, bq) + a_new * l
                m = m_new
            for u, base in enumerate(bases):
                if last:
                    rows_out = pl.ds(pl.multiple_of(base, bq), bq)
                    o_ref[rows_out, :] = (num[u] / l[u] * ga_ref[rows_out, :]).astype(BF16)
                else:
                    put(acc_ref, base, num[u])
                    put(m_ref, base, m[u])
                    put(l_ref, base, l[u])
            return carry

        lax.fori_loop(0, t // (bq * ATTN_UNROLL), body, 0)


def _attn_prompt(proj):
    b, t, _ = proj.shape
    for w, d in PATTERNS:
        assert w // d == KEYS_PER_PATTERN and t % (2 * KEYS_PER_PATTERN * d) == 0
    assert (t // KEYS_PER_PATTERN) % ATTN_UNROLL == 0
    col = lambda off: pl.BlockSpec((None, t, HEAD_DIM), lambda bi, h, off=off: (bi, 0, off + h))
    bq = KEYS_PER_PATTERN
    qi = jnp.arange(bq, dtype=jnp.int32)[:, None]
    kj = jnp.arange(2 * bq, dtype=jnp.int32)[None, :]
    visible = jnp.stack([kj <= qi, (kj >= qi) & (kj <= qi + bq)])
    bias = jnp.where(visible, 0.0, NEG_INF).astype(F32)
    keep = min(WINDOW_MAX, t)
    tail = pl.BlockSpec((None, keep, N_HEADS, HEAD_DIM), lambda bi, h: (bi, 0, 0, 0),
                        pipeline_mode=pl.Buffered(1))
    tail_shape = jax.ShapeDtypeStruct((b, keep, N_HEADS, HEAD_DIM), F32)
    return pl.pallas_call(
        _attn_prompt_kernel,
        grid=(b, N_HEADS),
        in_specs=[col(0), col(N_HEADS), col(2 * N_HEADS), col(3 * N_HEADS),
                  pl.BlockSpec((2, bq, 2 * bq), lambda bi, h: (0, 0, 0))],
        out_specs=(pl.BlockSpec((None, t, HEAD_DIM), lambda bi, h: (bi, 0, h)), tail, tail),
        out_shape=(jax.ShapeDtypeStruct((b, t, D_ATTN), BF16), tail_shape, tail_shape),
        scratch_shapes=[pltpu.VMEM((t, HEAD_DIM), F32)] * 3,
        compiler_params=_params("arbitrary", "arbitrary"),
        name="attn_prompt",
    )(proj, proj, proj, proj, bias)


def _pattern_count(delta):
    cnt = jnp.zeros(delta.shape, F32)
    for w, d in PATTERNS:
        assert d & (d - 1) == 0
        ok = (delta >= 0) & (delta <= w) & (jnp.bitwise_and(delta, d - 1) == 0)
        cnt = cnt + ok.astype(F32)
    return cnt


def _attn_sample_kernel(q_ref, kn_ref, vn_ref, ga_ref, ka_ref, kb_ref, va_ref, vb_ref, o_ref,
                        *, wb):
    tq = q_ref.shape[0]
    nh = N_HEADS
    assert tq & (tq - 1) == 0 and nh & (nh - 1) == 0
    tq_bits, nh_bits, sub_bits = (n.bit_length() - 1 for n in (tq, nh, SUBLANES))
    nq = nh * tq
    scale = HEAD_DIM ** -0.5

    def heads_to_rows(ref):
        return jnp.concatenate(
            [ref[:, h * HEAD_DIM:(h + 1) * HEAD_DIM] for h in range(nh)], axis=0)

    def flat(ref):
        g, j, h, d = ref.shape
        return ref[...].reshape(g * j * h, d).astype(BF16)

    def window_weights(n_groups, group0, residue0):
        shape = (nq, n_groups * SUBLANES * nh)
        r = lax.broadcasted_iota(jnp.int32, shape, 0)
        c = lax.broadcasted_iota(jnp.int32, shape, 1)
        key_head = c & (nh - 1)
        residue = residue0 + ((c >> nh_bits) & (SUBLANES - 1))
        key_row = (group0 + (c >> (nh_bits + sub_bits))) * SAMPLE_GROUP + residue
        delta = wb + (r & (tq - 1)) - key_row
        return jnp.where((r >> tq_bits) == key_head, _pattern_count(delta), 0.0)

    r = lax.broadcasted_iota(jnp.int32, (nq, nq), 0)
    c = lax.broadcasted_iota(jnp.int32, (nq, nq), 1)
    w_new = jnp.where((r >> tq_bits) == (c >> tq_bits),
                      _pattern_count((r & (tq - 1)) - (c & (tq - 1))), 0.0)
    n_dense = kb_ref.shape[0]
    parts = (
        (flat(ka_ref), flat(va_ref), window_weights(ka_ref.shape[0], 0, 0)),
        (flat(kb_ref), flat(vb_ref),
         window_weights(n_dense, wb // SAMPLE_GROUP - n_dense, SUBLANES)),
        (heads_to_rows(kn_ref).astype(BF16), heads_to_rows(vn_ref).astype(BF16), w_new),
    )
    q = heads_to_rows(q_ref).astype(BF16)
    scores = [jnp.where(wgt > 0, _dot_nt(q, k), NEG_INF) for k, _, wgt in parts]
    m = functools.reduce(jnp.maximum, [jnp.max(s, axis=-1, keepdims=True) for s in scores])
    den = jnp.zeros((nq, 1), F32)
    num = jnp.zeros((nq, HEAD_DIM), F32)
    for s, (_, v, wgt) in zip(scores, parts):
        p = wgt * jnp.exp2(s - m)
        den = den + jnp.sum(p, axis=-1, keepdims=True)
        num = num + _dot(p.astype(BF16), v)
    out = num / den
    for h in range(nh):
        cs = slice(h * HEAD_DIM, (h + 1) * HEAD_DIM)
        o_ref[:, cs] = out[h * tq:(h + 1) * tq] * ga_ref[:, cs]


def _attn_sample(proj, win_k, win_v, li, n_seq, tq):
    wb = win_k.shape[2]
    dense_reach = max(w for w, d in PATTERNS if d < SAMPLE_GROUP)
    assert max(d for _, d in PATTERNS) == SAMPLE_GROUP == 2 * SUBLANES and tq == SUBLANES
    assert wb % SAMPLE_GROUP == 0 and dense_reach % SAMPLE_GROUP == 0 and wb % dense_reach == 0
    n_groups = wb // SAMPLE_GROUP
    n_dense = dense_reach // SAMPLE_GROUP
    win_k = win_k.reshape(-1, n_seq, n_groups, SAMPLE_GROUP, N_HEADS, HEAD_DIM)
    win_v = win_v.reshape(-1, n_seq, n_groups, SAMPLE_GROUP, N_HEADS, HEAD_DIM)
    col = lambda blk: pl.BlockSpec((None, tq, D_ATTN), lambda s, blk=blk: (0, s, blk))
    sparse = pl.BlockSpec((None, None, n_groups, SUBLANES, N_HEADS, HEAD_DIM),
                          lambda s: (li, s, 0, 0, 0, 0))
    dense = pl.BlockSpec((None, None, n_dense, SUBLANES, N_HEADS, HEAD_DIM),
                         lambda s: (li, s, n_groups // n_dense - 1, 1, 0, 0))
    return pl.pallas_call(
        functools.partial(_attn_sample_kernel, wb=wb),
        grid=(n_seq,),
        in_specs=[col(0), col(1), col(2), col(3), sparse, dense, sparse, dense],
        out_specs=pl.BlockSpec((None, tq, D_ATTN), lambda s: (0, s, 0)),
        out_shape=jax.ShapeDtypeStruct((1, n_seq * tq, D_ATTN), F32),
        compiler_params=_params("arbitrary"),
        name="attn_sample",
    )(proj, proj, proj, proj, win_k, win_k, win_v, win_v)


def _pool_kernel(u_ref, halo_ref, pre_ref, gp_ref, w_ref, s_ref, o_ref, ext_ref, *, pos0):
    i = pl.program_id(1)
    g = pl.program_id(2)
    tm = u_ref.shape[0]
    npre = POOL_MAX

    @pl.when(i == 0)
    def _():
        ext_ref[0:npre, :] = pre_ref[...]

    @pl.when(i > 0)
    def _():
        ext_ref[0:npre, :] = halo_ref[...]

    u = u_ref[...]
    ext_ref[npre:, :] = u
    pos = pos0 + i * tm + lax.broadcasted_iota(jnp.int32, (tm, 1), 0)

    for gi, w in enumerate(POOL_WINDOWS):
        @pl.when(g == gi)
        def _(w=w):
            win = u
            for j in range(1, w):
                win = win + ext_ref[npre - j:npre - j + tm, :]
            inv_cnt = 1.0 / jnp.minimum(w, pos + 1).astype(F32)
            pooled = win * inv_cnt - u
            mixed = _dot(pooled.astype(BF16), w_ref[...].astype(BF16)) * s_ref[...]
            o_ref[...] = (mixed * gp_ref[...]).astype(o_ref.dtype)


def _pool(proj, prefix, pool_w, pool_scale, li, seq_rows, tm, pos0, out_dtype):
    g_, r_, _ = proj.shape
    n_seq = g_ * (r_ // seq_rows)
    nt = seq_rows // tm
    per_g = r_ // seq_rows
    ng = len(POOL_WINDOWS)
    u_blk0 = 4 * D_ATTN // POOL_GROUP
    gp_blk0 = u_blk0 + ng
    hb = tm // POOL_MAX

    def seq_map(blk0):
        return lambda s, i, g: (s // per_g, (s % per_g) * nt + i, blk0 + g)

    def halo_map(s, i, g):
        return (s // per_g, jnp.maximum(((s % per_g) * nt + i) * hb - 1, 0), u_blk0 + g)

    return pl.pallas_call(
        functools.partial(_pool_kernel, pos0=pos0),
        grid=(n_seq, nt, ng),
        in_specs=[
            pl.BlockSpec((None, tm, POOL_GROUP), seq_map(u_blk0)),
            pl.BlockSpec((None, POOL_MAX, POOL_GROUP), halo_map),
            pl.BlockSpec((None, POOL_MAX, POOL_GROUP), lambda s, i, g: (s, 0, g)),
            pl.BlockSpec((None, tm, POOL_GROUP), seq_map(gp_blk0)),
            pl.BlockSpec((None, None, POOL_GROUP, POOL_GROUP), lambda s, i, g: (li, g, 0, 0)),
            pl.BlockSpec((None, 1, POOL_GROUP), lambda s, i, g: (li, 0, g)),
        ],
        out_specs=pl.BlockSpec((None, tm, POOL_GROUP), seq_map(0)),
        out_shape=jax.ShapeDtypeStruct((g_, r_, D_POOL), out_dtype),
        scratch_shapes=[pltpu.VMEM((POOL_MAX + tm, POOL_GROUP), F32)],
        compiler_params=_params("arbitrary", "arbitrary", "arbitrary"),
        name="pool_mixer",
    )(proj, proj, prefix, proj, pool_w, pool_scale)


def _final_norm_kernel(x_ref, g_ref, o_ref):
    x = x_ref[...]
    ms = jnp.mean(x * x, axis=-1, keepdims=True)
    o_ref[...] = x * lax.rsqrt(ms + RMS_EPS) * g_ref[...]


def _final_norm(x, g, bb, tm):
    bk, t, d = x.shape
    return pl.pallas_call(
        _final_norm_kernel,
        grid=(bk // bb, t // tm),
        in_specs=[pl.BlockSpec((bb, tm, d), lambda b, i: (b, i, 0)),
                  pl.BlockSpec((1, d), lambda b, i: (0, 0))],
        out_specs=pl.BlockSpec((bb, tm, d), lambda b, i: (b, i, 0)),
        out_shape=jax.ShapeDtypeStruct((bk, t, d), F32),
        compiler_params=_params("arbitrary", "arbitrary"),
        name="final_norm",
    )(x, g)


def _trunk(x, mods, mod_row0, pos0, state, weights, bb, tm, tm_out):
    (norm_g, w_in_even, w_out_even, pool_w, pool_scale, w_in_odd, conv_w, w_out_odd,
     final_g) = weights
    bk, t, _ = x.shape
    fresh = state is None
    assert mod_row0 % bb == 0
    cos, sin = _rope_tables(t, pos0)
    if bb > 1:
        cos, sin = jnp.tile(cos, (bb, 1)), jnp.tile(sin, (bb, 1))
    coef = _rotation_coefficients(cos, sin)
    new_k, new_v, new_pool, new_conv = [], [], [], []
    for layer in range(DEPTH):
        li = layer // 2
        where = (layer, li, mod_row0 // bb)
        if layer % 2 == 0:
            proj = _in_even(x, mods, norm_g, w_in_even, coef, where, bb, tm)
            seq = proj.reshape(bk, t, D_IN_EVEN)
            u = seq[:, :, 4 * D_ATTN:4 * D_ATTN + D_POOL]
            if fresh:
                pool_prefix = jnp.zeros((bk, POOL_MAX - 1, D_POOL), F32)
                attn, k, v = _attn_prompt(proj)
                new_pool.append(u[:, -(POOL_MAX - 1):])
            else:
                win_k, win_v, pool_st, _ = state
                pool_prefix = pool_st[li]
                attn = _attn_sample(proj, win_k, win_v, li, bk, t)
                k = seq[:, :, D_ATTN:2 * D_ATTN].reshape(bk, t, N_HEADS, HEAD_DIM)
                v = seq[:, :, 2 * D_ATTN:3 * D_ATTN].reshape(bk, t, N_HEADS, HEAD_DIM)
                new_pool.append(
                    jnp.concatenate([pool_prefix, u], axis=1)[:, -(POOL_MAX - 1):])
            new_k.append(k)
            new_v.append(v)
            prefix = jnp.pad(pool_prefix, ((0, 0), (1, 0), (0, 0)))
            pooled = _pool(proj, prefix, pool_w, pool_scale, li, t,
                           min(t, 1024), pos0, BF16 if fresh else F32)
            x = _out_proj(attn, 0, pooled, 0, w_out_even, x, mods, where, bb, tm_out)
        else:
            if fresh:
                prefix = jnp.zeros((bk, SUBLANES, D_CONV), F32)
            else:
                prefix = jnp.pad(state[3][li], ((0, 0), (SUBLANES - (CONV_WIDTH - 1), 0), (0, 0)))
            y, tail = _in_odd(x, mods, norm_g, w_in_odd, conv_w, prefix, where, bb, tm)
            new_conv.append(tail[:, SUBLANES - (CONV_WIDTH - 1):])
            x = _out_proj(y, 0, y, 1, w_out_odd, x, mods, where, bb, tm_out)
    y = _final_norm(x, final_g[None, :], bb, tm)
    return y, jnp.stack(new_k), jnp.stack(new_v), jnp.stack(new_pool), jnp.stack(new_conv)


def kernel(x_prompt, x_sample, c_prompt, c_sample, state_win_k, state_win_v, state_pool,
           state_conv, ada_w, ada_b, norm_g, w_in_even, w_out_even, pool_w, pool_scale,
           w_in_odd, conv_w, w_out_odd, final_g):
    nb, t_p, _ = x_prompt.shape
    ns, t_s, _ = x_sample.shape
    n_c = nb + ns
    c_rows = -(-n_c // SUBLANES) * SUBLANES
    c_all = jnp.pad(jnp.concatenate([c_sample, c_prompt], axis=0), ((0, c_rows - n_c), (0, 0)))
    mods = _ada_modulation(c_all, ada_w, ada_b)[:, :, None, :]
    weights = (norm_g[:, None, :], w_in_even.astype(BF16), w_out_even.astype(BF16), pool_w,
               pool_scale[:, None, :], w_in_odd.astype(BF16), conv_w, w_out_odd.astype(BF16),
               final_g)

    y_p, kp, vp, poolp, convp = _trunk(x_prompt, mods, ns, 0, None, weights, 1, 1024, 2048)

    state = (state_win_k, state_win_v, state_pool, state_conv)
    y_s, ksm, vsm, pools, convs = _trunk(x_sample, mods, 0, PAST_LEN, state, weights, ns, t_s,
                                         t_s)
    return (y_p, y_s, kp, vp, poolp, convp, ksm, vsm, pools, convs)
```

```python
import functools
import math

import jax
import jax.numpy as jnp
from jax import lax
from jax.experimental import pallas as pl
from jax.experimental.pallas import tpu as pltpu

F32 = jnp.float32
BF16 = jnp.bfloat16

D_MODEL = 2048
DEPTH = 4
PAST_LEN = 16384
D_ATTN = D_MODEL // 2
HEAD_DIM = 128
N_HEADS = D_ATTN // HEAD_DIM
PATTERNS = ((128, 1), (512, 4), (2048, 16))
KEYS_PER_PATTERN = 128
ATTN_UNROLL = 8
SAMPLE_GROUP = max(d for _, d in PATTERNS)
MATMUL_ROW_CHUNK = 128
WINDOW_MAX = max(w for w, _ in PATTERNS)
ROPE_THETA = 10000.0
D_POOL = D_MODEL // 2
POOL_WINDOWS = (2, 4, 8, 16)
POOL_GROUP = D_POOL // len(POOL_WINDOWS)
POOL_MAX = max(POOL_WINDOWS)
D_CONV = D_MODEL
CONV_WIDTH = 3
D_IN_EVEN = 4 * D_ATTN + 2 * D_POOL
RMS_EPS = 1e-6
NEG_INF = -1e30
QK_SCALE = HEAD_DIM ** -0.5 * math.log2(math.e)

LANES = 128
SUBLANES = 8
VMEM_LIMIT = 48 * 1024 * 1024


def _params(*semantics):
    return pltpu.CompilerParams(dimension_semantics=semantics,
                                vmem_limit_bytes=VMEM_LIMIT)


def _silu(v):
    return v / (1.0 + jnp.exp(-v))


def _dot(a, b):
    return jnp.dot(a, b, preferred_element_type=F32)


def _dot_nt(a, b):
    return lax.dot_general(a, b, (((1,), (1,)), ((), ())), preferred_element_type=F32)


def _rope_kernel(inv_ref, cos_ref, sin_ref, *, pos0):
    t = cos_ref.shape[0]
    pos = (lax.broadcasted_iota(jnp.int32, (t, HEAD_DIM), 0) + pos0).astype(F32)
    lane = lax.broadcasted_iota(jnp.int32, (t, HEAD_DIM), 1)
    ang = pos * inv_ref[...]
    s = jnp.sin(ang)
    cos_ref[...] = jnp.cos(ang)
    sin_ref[...] = jnp.where(lane < HEAD_DIM // 2, -s, s)


def _rope_tables(t, pos0):
    half = HEAD_DIM // 2
    inv = jnp.power(ROPE_THETA, -jnp.arange(half, dtype=F32) / half)
    inv = jnp.concatenate([inv, inv])[None, :]
    return pl.pallas_call(
        functools.partial(_rope_kernel, pos0=pos0),
        out_shape=(jax.ShapeDtypeStruct((t, HEAD_DIM), F32),
                   jax.ShapeDtypeStruct((t, HEAD_DIM), F32)),
        name="rope_tables",
    )(inv)


def _ada_kernel(c_ref, w_ref, b_ref, o_ref):
    a = _silu(c_ref[...]).astype(BF16)
    o_ref[...] = _dot(a, w_ref[...].astype(BF16)) + b_ref[...]


def _ada_modulation(c_all, ada_w, ada_b):
    rows = c_all.shape[0]
    n_out = ada_w.shape[2]
    tn = 1024
    return pl.pallas_call(
        _ada_kernel,
        grid=(DEPTH, n_out // tn),
        in_specs=[
            pl.BlockSpec((rows, D_MODEL), lambda l, n: (0, 0)),
            pl.BlockSpec((None, D_MODEL, tn), lambda l, n: (l, 0, n)),
            pl.BlockSpec((None, 1, tn), lambda l, n: (l, 0, n)),
        ],
        out_specs=pl.BlockSpec((None, rows, tn), lambda l, n: (l, 0, n)),
        out_shape=jax.ShapeDtypeStruct((DEPTH, rows, n_out), F32),
        compiler_params=_params("arbitrary", "arbitrary"),
        name="ada_modulation",
    )(c_all, ada_w, ada_b.reshape(DEPTH, 1, n_out))


def _row_chunks(bb, tm):
    chunk = min(tm, MATMUL_ROW_CHUNK) if bb == 1 else bb * tm
    return [slice(c * chunk, (c + 1) * chunk) for c in range(bb * tm // chunk)]


def _normalized_rows(first, rs, bb, x_ref, shift_ref, scale_ref, g_ref, h_ref):
    if not first:
        return h_ref[rs, :]
    h = _modulated_norm(x_ref, shift_ref, scale_ref, g_ref, rs if bb == 1 else None)
    h_ref[rs, :] = h
    return h


def _modulated_norm(x_ref, shift_ref, scale_ref, g_ref, rows=None):
    x = x_ref[...] if rows is None else x_ref[:, rows, :]
    ms = jnp.mean(x * x, axis=-1, keepdims=True)
    y = x * lax.rsqrt(ms + RMS_EPS) * g_ref[...]
    h = y * (1.0 + scale_ref[...]) + shift_ref[...]
    bb, tm, d = h.shape
    return h.reshape(bb * tm, d).astype(BF16)


def _bf16_weights(w_ref, emit_ref):
    if emit_ref is None:
        return w_ref[...]
    w = w_ref[...].astype(BF16)
    emit_ref[...] = w
    return w


def _in_even_kernel(x_ref, shift_ref, scale_ref, g_ref, w_ref, cos_ref, sin_ref,
                    o_ref, *rest):
    emit_ref, h_ref = rest if len(rest) == 2 else (None, rest[0])
    n = pl.program_id(2)
    tn = w_ref.shape[1]
    per = D_ATTN // tn
    assert D_POOL == D_ATTN and D_ATTN % tn == 0

    is_gate = ((n >= 3 * per) & (n < 4 * per)) | (n >= 5 * per)
    bb, tm, _ = x_ref.shape

    def step(first):
        for rs in _row_chunks(bb, tm):
            h = _normalized_rows(first, rs, bb, x_ref, shift_ref, scale_ref, g_ref, h_ref)
            acc = _dot(h, _bf16_weights(w_ref, emit_ref))
            cos = cos_ref[rs, :]
            sin = sin_ref[rs, :]
            for j in range(tn // HEAD_DIM):
                cs = slice(j * HEAD_DIM, (j + 1) * HEAD_DIM)
                a = acc[:, cs]
                lin = a * cos + pltpu.roll(a, HEAD_DIM // 2, axis=1) * sin
                o_ref[rs, cs] = jnp.where(is_gate, _silu(lin), lin)

    pl.when(n == 0)(functools.partial(step, True))
    pl.when(n > 0)(functools.partial(step, False))


def _mod_spec(where, bb, width, col_blk):
    layer, _, mod_blk0 = where
    return pl.BlockSpec((None, bb, 1, width),
                        lambda b, i, n: (layer, mod_blk0 + b, 0, col_blk(n)))


def _rotation_coefficients(cos_rows, sin_rows):
    ones, zeros = jnp.ones_like(cos_rows), jnp.zeros_like(sin_rows)
    return jnp.stack([jnp.stack([cos_rows * QK_SCALE, sin_rows * QK_SCALE]),
                      jnp.stack([cos_rows, sin_rows]),
                      jnp.stack([ones, zeros])])


def _weight_specs(w, li, rows, cols, col_blk):
    flat = lambda b, i, n: (0, col_blk(n))
    if w.ndim == 2:
        return pl.BlockSpec((rows, cols), flat), None, None
    assert w.dtype == F32 and w.shape[1] == rows
    return (pl.BlockSpec((None, rows, cols), lambda b, i, n: (li, 0, col_blk(n))),
            pl.BlockSpec((rows, cols), flat), jax.ShapeDtypeStruct((rows, w.shape[2]), BF16))


def _in_even(x, mods, norm_g, w, coef, where, bb, tm):
    layer, li, _ = where
    bk, t, d = x.shape
    n_out = w.shape[-1]
    tn = 512
    per = D_ATTN // tn
    rows = bb * tm
    grid = (bk // bb, t // tm, n_out // tn)
    coef_spec = lambda which: pl.BlockSpec(
        (None, None, rows, HEAD_DIM),
        lambda b, i, n: (jnp.minimum(n // per, 2), which, i, 0))
    w_spec, emit_spec, emit_shape = _weight_specs(w, li, d, tn, lambda n: n)
    emit = emit_spec is not None
    assert not emit or grid[:2] == (1, 1)
    out_specs = [pl.BlockSpec((None, rows, tn), lambda b, i, n: (b, i, n))]
    out_shape = [jax.ShapeDtypeStruct((bk // bb, (t // tm) * rows, n_out), F32)]
    outs = pl.pallas_call(
        _in_even_kernel,
        grid=grid,
        in_specs=[
            pl.BlockSpec((bb, tm, d), lambda b, i, n: (b, i, 0)),
            _mod_spec(where, bb, d, lambda n: 0),
            _mod_spec(where, bb, d, lambda n: 1),
            pl.BlockSpec((None, 1, d), lambda b, i, n: (layer, 0, 0)),
            w_spec,
            coef_spec(0),
            coef_spec(1),
        ],
        out_specs=out_specs + ([emit_spec] if emit else []),
        out_shape=out_shape + ([emit_shape] if emit else []),
        scratch_shapes=[pltpu.VMEM((rows, d), BF16)],
        compiler_params=_params("arbitrary", "arbitrary", "arbitrary"),
        name="in_proj_even",
    )(x, mods, mods, norm_g, w, coef, coef)
    return outs[0], (outs[1] if emit else None)


def _in_odd_kernel(x_ref, shift_ref, scale_ref, g_ref, wb_ref, wc_ref, wh_ref, wg_ref,
                   cw_ref, pre_ref, o_ref, tail_ref, *rest):
    *emit_refs, h_ref, zbuf_ref = rest
    eb_ref, ec_ref, eh_ref, eg_ref = emit_refs if emit_refs else (None,) * 4
    i = pl.program_id(1)
    n = pl.program_id(2)
    bb, tm, _ = x_ref.shape
    tn = wb_ref.shape[1]
    halo = SUBLANES
    tail_cols = pl.ds(pl.multiple_of(n * tn, tn), tn)

    @pl.when((n == 0) & (i == 0))
    def _():
        tail_ref[...] = pre_ref[...]

    def step(first):
        zbuf_ref[:, 0:halo, :] = tail_ref[:, :, tail_cols]
        cw = cw_ref[...]
        for rs in _row_chunks(bb, tm):
            lo, hi = (rs.start, rs.stop) if bb == 1 else (0, tm)
            h = _normalized_rows(first, rs, bb, x_ref, shift_ref, scale_ref, g_ref, h_ref)
            z = (_dot(h, _bf16_weights(wc_ref, ec_ref))
                 * _dot(h, _bf16_weights(wh_ref, eh_ref))).reshape(bb, hi - lo, tn)
            zbuf_ref[:, halo + lo:halo + hi, :] = z
            yc = zbuf_ref[:, halo - 2 + lo:halo - 2 + hi, :] * cw[0:1]
            yc = yc + zbuf_ref[:, halo - 1 + lo:halo - 1 + hi, :] * cw[1:2]
            yc = yc + z * cw[2:3]
            bg = _dot(h, _bf16_weights(wb_ref, eb_ref)).reshape(bb, hi - lo, tn)
            gate = _dot(h, _bf16_weights(wg_ref, eg_ref)).reshape(bb, hi - lo, tn)
            y = bg * yc * _silu(gate)
            o_ref[rs, :] = y.reshape(rs.stop - rs.start, tn).astype(BF16)
        tail_ref[:, :, tail_cols] = zbuf_ref[:, tm:tm + halo, :]

    pl.when(n == 0)(functools.partial(step, True))
    pl.when(n > 0)(functools.partial(step, False))


def _in_odd(x, mods, norm_g, w, conv_w, prefix, where, bb, tm):
    layer, li, _ = where
    bk, t, d = x.shape
    tn = 256
    nt = D_CONV // tn
    rows = bb * tm
    grid = (bk // bb, t // tm, nt)
    emit = not isinstance(w, (tuple, list))
    assert not emit or grid[:2] == (1, 1)
    if emit:
        specs = [_weight_specs(w, li, d, tn, lambda n, k=k: k * nt + n) for k in range(4)]
        emit_specs = [pl.BlockSpec((d, tn), lambda b, i, n: (0, n))] * 4
        emit_shapes = [jax.ShapeDtypeStruct((d, D_CONV), BF16)] * 4
        w_args = (w,) * 4
    else:
        specs = [_weight_specs(wk, li, d, tn, lambda n: n) for wk in w]
        emit_specs, emit_shapes = [], []
        w_args = tuple(w)
    outs = pl.pallas_call(
        _in_odd_kernel,
        grid=grid,
        in_specs=[
            pl.BlockSpec((bb, tm, d), lambda b, i, n: (b, i, 0)),
            _mod_spec(where, bb, d, lambda n: 0),
            _mod_spec(where, bb, d, lambda n: 1),
            pl.BlockSpec((None, 1, d), lambda b, i, n: (layer, 0, 0)),
            *[s[0] for s in specs],
            pl.BlockSpec((None, CONV_WIDTH, tn), lambda b, i, n: (li, 0, n)),
            pl.BlockSpec((bb, SUBLANES, D_CONV), lambda b, i, n: (b, 0, 0)),
        ],
        out_specs=[
            pl.BlockSpec((None, rows, tn), lambda b, i, n: (b, i, n)),
            pl.BlockSpec((bb, SUBLANES, D_CONV), lambda b, i, n: (b, 0, 0)),
        ] + emit_specs,
        out_shape=[
            jax.ShapeDtypeStruct((bk // bb, (t // tm) * rows, D_CONV), BF16),
            jax.ShapeDtypeStruct((bk, SUBLANES, D_CONV), F32),
        ] + emit_shapes,
        scratch_shapes=[
            pltpu.VMEM((rows, d), BF16),
            pltpu.VMEM((bb, SUBLANES + tm, tn), F32),
        ],
        compiler_params=_params("arbitrary", "arbitrary", "arbitrary"),
        name="in_proj_odd_conv",
    )(x, mods, mods, norm_g, *w_args, conv_w, prefix)
    return outs[0], outs[1], (tuple(outs[2:]) if emit else None)


def _out_proj_kernel(a1_ref, a2_ref, w1_ref, w2_ref, x_ref, gate_ref, fg_ref, o_ref,
                     *emit_refs, final):
    e1_ref, e2_ref = emit_refs if emit_refs else (None, None)
    bb, tm, d = x_ref.shape
    for rs in _row_chunks(bb, tm):
        lo, hi = (rs.start, rs.stop) if bb == 1 else (0, tm)
        acc = _dot(a1_ref[rs, :].astype(BF16), _bf16_weights(w1_ref, e1_ref))
        acc = acc + _dot(a2_ref[rs, :].astype(BF16), _bf16_weights(w2_ref, e2_ref))
        x = x_ref[:, lo:hi, :] + gate_ref[...] * acc.reshape(bb, hi - lo, d)
        if final:
            ms = jnp.mean(x * x, axis=-1, keepdims=True)
            x = x * lax.rsqrt(ms + RMS_EPS) * fg_ref[...]
        o_ref[:, lo:hi, :] = x


def _out_proj(a1, a1_blk, a2, a2_blk, w, x, mods, final_g, where, bb, tm):
    layer, li, mod_blk0 = where
    bk, t, d = x.shape
    rows = bb * tm
    grid = (bk // bb, t // tm)
    emit = not isinstance(w, (tuple, list))
    assert not emit or grid == (1, 1)
    if emit:
        kh = w.shape[1] // 2
        w_specs = [pl.BlockSpec((None, kh, d), lambda b, i, k=k: (li, k, 0)) for k in range(2)]
        emit_specs = [pl.BlockSpec((kh, d), lambda b, i: (0, 0))] * 2
        emit_shapes = [jax.ShapeDtypeStruct((kh, d), BF16)] * 2
        w_args = (w, w)
    else:
        kh = w[0].shape[0]
        w_specs = [pl.BlockSpec((kh, d), lambda b, i: (0, 0))] * 2
        emit_specs, emit_shapes = [], []
        w_args = tuple(w)
    outs = pl.pallas_call(
        functools.partial(_out_proj_kernel, final=layer == DEPTH - 1),
        grid=grid,
        in_specs=[
            pl.BlockSpec((None, rows, kh), lambda b, i: (b, i, a1_blk)),
            pl.BlockSpec((None, rows, kh), lambda b, i: (b, i, a2_blk)),
            *w_specs,
            pl.BlockSpec((bb, tm, d), lambda b, i: (b, i, 0)),
            pl.BlockSpec((None, bb, 1, d), lambda b, i: (layer, mod_blk0 + b, 0, 2)),
            pl.BlockSpec((1, d), lambda b, i: (0, 0)),
        ],
        out_specs=[pl.BlockSpec((bb, tm, d), lambda b, i: (b, i, 0))] + emit_specs,
        out_shape=[jax.ShapeDtypeStruct((bk, t, d), F32)] + emit_shapes,
        compiler_params=_params("arbitrary", "arbitrary"),
        name="out_proj_residual",
    )(a1, a2, *w_args, x, mods, final_g)
    return outs[0], (tuple(outs[1:]) if emit else None)


def _attn_prompt_kernel(q_ref, k_ref, v_ref, ga_ref, bias_ref, o_ref, kt_ref, vt_ref,
                        acc_ref, m_ref, l_ref):
    t = q_ref.shape[0]
    bq = KEYS_PER_PATTERN
    dilations = sorted((d for _, d in PATTERNS), reverse=True)
    assert dilations[-1] == 1

    keep, nh, _ = kt_ref.shape
    head = pl.program_id(1)
    for src_ref, dst_ref in ((k_ref, kt_ref), (v_ref, vt_ref)):
        dst = dst_ref.reshape(keep * nh, HEAD_DIM)
        dst[pl.ds(head, keep, stride=nh), :] = src_ref[pl.ds(t - keep, keep), :]

    for pi, d in enumerate(dilations):
        span = bq * d
        nblk = t // span
        first = pi == 0
        last = pi == len(dilations) - 1

        def rows(ref, start, n, d=d):
            if d == 1:
                return ref[pl.ds(pl.multiple_of(start, bq), n), :]
            return ref[pl.ds(start, n, stride=d), :]

        def put(ref, start, val, d=d):
            if d == 1:
                ref[pl.ds(pl.multiple_of(start, bq), bq), :] = val
            else:
                ref[pl.ds(start, bq, stride=d), :] = val

        def body(it, carry, nblk=nblk, span=span, rows=rows, put=put, first=first, last=last):
            bases, kbases, has_prevs = [], [], []
            for u in range(ATTN_UNROLL):
                itu = it * ATTN_UNROLL + u
                blk = itu % nblk
                base = itu // nblk + blk * span
                has_prev = jnp.minimum(blk, 1)
                bases.append(base)
                kbases.append(base - has_prev * span)
                has_prevs.append(has_prev)
            stack = lambda ref, starts, n: jnp.stack([rows(ref, s, n) for s in starts])

            def keys(ref):
                if nblk == 2 and ATTN_UNROLL % 2 == 0:
                    pairs = [rows(ref, s, 2 * bq).astype(BF16) for s in kbases[::2]]
                    return jnp.stack([pairs[u // 2] for u in range(ATTN_UNROLL)])
                return stack(ref, kbases, 2 * bq).astype(BF16)

            q = stack(q_ref, bases, bq).astype(BF16)
            k = keys(k_ref)
            v = keys(v_ref)
            s = jnp.einsum('uqd,ukd->uqk', q, k, preferred_element_type=F32)
            s = s + jnp.stack([bias_ref[hp] for hp in has_prevs])
            m = jnp.max(jnp.maximum(s[:, :, :bq], s[:, :, bq:]), axis=-1, keepdims=True)
            p = jnp.exp2(s - m)
            l = jnp.sum(p[:, :, :bq] + p[:, :, bq:], axis=-1, keepdims=True)
            num = jnp.einsum('uqk,ukd->uqd', p.astype(BF16), v, preferred_element_type=F32)
            m = jnp.broadcast_to(m, num.shape)
            l = jnp.broadcast_to(l, num.shape)
            if not first:
                m_old = stack(m_ref, bases, bq)
                m_new = jnp.maximum(m_old, m)
                a_old = jnp.exp2(m_old - m_new)
                a_new = jnp.exp2(m - m_new)
                num = a_old * stack(acc_ref, bases, bq) + a_new * num
                l = a_old * stack(l_ref, bases, bq) + a_new * l
                m = m_new
            for u, base in enumerate(bases):
                if last:
                    rows_out = pl.ds(pl.multiple_of(base, bq), bq)
                    o_ref[rows_out, :] = (num[u] / l[u] * ga_ref[rows_out, :]).astype(BF16)
                else:
                    put(acc_ref, base, num[u])
                    put(m_ref, base, m[u])
                    put(l_ref, base, l[u])
            return carry

        lax.fori_loop(0, t // (bq * ATTN_UNROLL), body, 0)


def _attn_prompt(proj):
    b, t, _ = proj.shape
    for w, d in PATTERNS:
        assert w // d == KEYS_PER_PATTERN and t % (2 * KEYS_PER_PATTERN * d) == 0
    assert (t // KEYS_PER_PATTERN) % ATTN_UNROLL == 0
    col = lambda off: pl.BlockSpec((None, t, HEAD_DIM), lambda bi, h, off=off: (bi, 0, off + h))
    bq = KEYS_PER_PATTERN
    qi = jnp.arange(bq, dtype=jnp.int32)[:, None]
    kj = jnp.arange(2 * bq, dtype=jnp.int32)[None, :]
    visible = jnp.stack([kj <= qi, (kj >= qi) & (kj <= qi + bq)])
    bias = jnp.where(visible, 0.0, NEG_INF).astype(F32)
    keep = min(WINDOW_MAX, t)
    tail = pl.BlockSpec((None, keep, N_HEADS, HEAD_DIM), lambda bi, h: (bi, 0, 0, 0),
                        pipeline_mode=pl.Buffered(1))
    tail_shape = jax.ShapeDtypeStruct((b, keep, N_HEADS, HEAD_DIM), F32)
    return pl.pallas_call(
        _attn_prompt_kernel,
        grid=(b, N_HEADS),
        in_specs=[col(0), col(N_HEADS), col(2 * N_HEADS), col(3 * N_HEADS),
                  pl.BlockSpec((2, bq, 2 * bq), lambda bi, h: (0, 0, 0))],
        out_specs=(pl.BlockSpec((None, t, HEAD_DIM), lambda bi, h: (bi, 0, h)), tail, tail),
        out_shape=(jax.ShapeDtypeStruct((b, t, D_ATTN), BF16), tail_shape, tail_shape),
        scratch_shapes=[pltpu.VMEM((t, HEAD_DIM), F32)] * 3,
        compiler_params=_params("arbitrary", "arbitrary"),
        name="attn_prompt",
    )(proj, proj, proj, proj, bias)


def _pattern_count(delta):
    cnt = jnp.zeros(delta.shape, F32)
    for w, d in PATTERNS:
        assert d & (d - 1) == 0
        ok = (delta >= 0) & (delta <= w) & (jnp.bitwise_and(delta, d - 1) == 0)
        cnt = cnt + ok.astype(F32)
    return cnt


def _attn_sample_kernel(q_ref, kn_ref, vn_ref, ga_ref, ka_ref, kb_ref, va_ref, vb_ref, o_ref,
                        *, wb):
    tq = q_ref.shape[0]
    nh = N_HEADS
    assert tq & (tq - 1) == 0 and nh & (nh - 1) == 0
    tq_bits, nh_bits, sub_bits = (n.bit_length() - 1 for n in (tq, nh, SUBLANES))
    nq = nh * tq
    scale = HEAD_DIM ** -0.5

    def heads_to_rows(ref):
        return jnp.concatenate(
            [ref[:, h * HEAD_DIM:(h + 1) * HEAD_DIM] for h in range(nh)], axis=0)

    def flat(ref):
        g, j, h, d = ref.shape
        return ref[...].reshape(g * j * h, d).astype(BF16)

    def window_weights(n_groups, group0, residue0):
        shape = (nq, n_groups * SUBLANES * nh)
        r = lax.broadcasted_iota(jnp.int32, shape, 0)
        c = lax.broadcasted_iota(jnp.int32, shape, 1)
        key_head = c & (nh - 1)
        residue = residue0 + ((c >> nh_bits) & (SUBLANES - 1))
        key_row = (group0 + (c >> (nh_bits + sub_bits))) * SAMPLE_GROUP + residue
        delta = wb + (r & (tq - 1)) - key_row
        return jnp.where((r >> tq_bits) == key_head, _pattern_count(delta), 0.0)

    r = lax.broadcasted_iota(jnp.int32, (nq, nq), 0)
    c = lax.broadcasted_iota(jnp.int32, (nq, nq), 1)
    w_new = jnp.where((r >> tq_bits) == (c >> tq_bits),
                      _pattern_count((r & (tq - 1)) - (c & (tq - 1))), 0.0)
    n_dense = kb_ref.shape[0]
    parts = (
        (flat(ka_ref), flat(va_ref), window_weights(ka_ref.shape[0], 0, 0)),
        (flat(kb_ref), flat(vb_ref),
         window_weights(n_dense, wb // SAMPLE_GROUP - n_dense, SUBLANES)),
        (heads_to_rows(kn_ref).astype(BF16), heads_to_rows(vn_ref).astype(BF16), w_new),
    )
    q = heads_to_rows(q_ref).astype(BF16)
    scores = [jnp.where(wgt > 0, _dot_nt(q, k), NEG_INF) for k, _, wgt in parts]
    m = functools.reduce(jnp.maximum, [jnp.max(s, axis=-1, keepdims=True) for s in scores])
    den = jnp.zeros((nq, 1), F32)
    num = jnp.zeros((nq, HEAD_DIM), F32)
    for s, (_, v, wgt) in zip(scores, parts):
        p = wgt * jnp.exp2(s - m)
        den = den + jnp.sum(p, axis=-1, keepdims=True)
        num = num + _dot(p.astype(BF16), v)
    out = num / den
    for h in range(nh):
        cs = slice(h * HEAD_DIM, (h + 1) * HEAD_DIM)
        o_ref[:, cs] = out[h * tq:(h + 1) * tq] * ga_ref[:, cs]


def _attn_sample(proj, win_k, win_v, li, n_seq, tq):
    wb = win_k.shape[2]
    dense_reach = max(w for w, d in PATTERNS if d < SAMPLE_GROUP)
    assert max(d for _, d in PATTERNS) == SAMPLE_GROUP == 2 * SUBLANES and tq == SUBLANES
    assert wb % SAMPLE_GROUP == 0 and dense_reach % SAMPLE_GROUP == 0 and wb % dense_reach == 0
    n_groups = wb // SAMPLE_GROUP
    n_dense = dense_reach // SAMPLE_GROUP
    win_k = win_k.reshape(-1, n_seq, n_groups, SAMPLE_GROUP, N_HEADS, HEAD_DIM)
    win_v = win_v.reshape(-1, n_seq, n_groups, SAMPLE_GROUP, N_HEADS, HEAD_DIM)
    col = lambda blk: pl.BlockSpec((None, tq, D_ATTN), lambda s, blk=blk: (0, s, blk))
    sparse = pl.BlockSpec((None, None, n_groups, SUBLANES, N_HEADS, HEAD_DIM),
                          lambda s: (li, s, 0, 0, 0, 0))
    dense = pl.BlockSpec((None, None, n_dense, SUBLANES, N_HEADS, HEAD_DIM),
                         lambda s: (li, s, n_groups // n_dense - 1, 1, 0, 0))
    return pl.pallas_call(
        functools.partial(_attn_sample_kernel, wb=wb),
        grid=(n_seq,),
        in_specs=[col(0), col(1), col(2), col(3), sparse, dense, sparse, dense],
        out_specs=pl.BlockSpec((None, tq, D_ATTN), lambda s: (0, s, 0)),
        out_shape=jax.ShapeDtypeStruct((1, n_seq * tq, D_ATTN), F32),
        compiler_params=_params("arbitrary"),
        name="attn_sample",
    )(proj, proj, proj, proj, win_k, win_k, win_v, win_v)


def _pool_kernel(u_ref, halo_ref, pre_ref, gp_ref, w_ref, s_ref, o_ref, ext_ref, *, pos0):
    i = pl.program_id(1)
    g = pl.program_id(2)
    tm = u_ref.shape[0]
    npre = POOL_MAX

    @pl.when(i == 0)
    def _():
        ext_ref[0:npre, :] = pre_ref[...]

    @pl.when(i > 0)
    def _():
        ext_ref[0:npre, :] = halo_ref[...]

    u = u_ref[...]
    ext_ref[npre:, :] = u
    pos = pos0 + i * tm + lax.broadcasted_iota(jnp.int32, (tm, 1), 0)

    for gi, w in enumerate(POOL_WINDOWS):
        @pl.when(g == gi)
        def _(w=w):
            win = u
            for j in range(1, w):
                win = win + ext_ref[npre - j:npre - j + tm, :]
            inv_cnt = 1.0 / jnp.minimum(w, pos + 1).astype(F32)
            pooled = win * inv_cnt - u
            mixed = _dot(pooled.astype(BF16), w_ref[...].astype(BF16)) * s_ref[...]
            o_ref[...] = (mixed * gp_ref[...]).astype(o_ref.dtype)


def _pool(proj, prefix, pool_w, pool_scale, li, seq_rows, tm, pos0, out_dtype):
    g_, r_, _ = proj.shape
    n_seq = g_ * (r_ // seq_rows)
    nt = seq_rows // tm
    per_g = r_ // seq_rows
    ng = len(POOL_WINDOWS)
    u_blk0 = 4 * D_ATTN // POOL_GROUP
    gp_blk0 = u_blk0 + ng
    hb = tm // POOL_MAX

    def seq_map(blk0):
        return lambda s, i, g: (s // per_g, (s % per_g) * nt + i, blk0 + g)

    def halo_map(s, i, g):
        return (s // per_g, jnp.maximum(((s % per_g) * nt + i) * hb - 1, 0), u_blk0 + g)

    return pl.pallas_call(
        functools.partial(_pool_kernel, pos0=pos0),
        grid=(n_seq, nt, ng),
        in_specs=[
            pl.BlockSpec((None, tm, POOL_GROUP), seq_map(u_blk0)),
            pl.BlockSpec((None, POOL_MAX, POOL_GROUP), halo_map),
            pl.BlockSpec((None, POOL_MAX, POOL_GROUP), lambda s, i, g: (s, 0, g)),
            pl.BlockSpec((None, tm, POOL_GROUP), seq_map(gp_blk0)),
            pl.BlockSpec((None, None, POOL_GROUP, POOL_GROUP), lambda s, i, g: (li, g, 0, 0)),
            pl.BlockSpec((None, 1, POOL_GROUP), lambda s, i, g: (li, 0, g)),
        ],
        out_specs=pl.BlockSpec((None, tm, POOL_GROUP), seq_map(0)),
        out_shape=jax.ShapeDtypeStruct((g_, r_, D_POOL), out_dtype),
        scratch_shapes=[pltpu.VMEM((POOL_MAX + tm, POOL_GROUP), F32)],
        compiler_params=_params("arbitrary", "arbitrary", "arbitrary"),
        name="pool_mixer",
    )(proj, proj, prefix, proj, pool_w, pool_scale)


def _trunk(x, mods, mod_row0, pos0, state, weights, matmul_weights, bb, tm, tm_out):
    norm_g, pool_w, pool_scale, conv_w, final_g = weights
    bk, t, _ = x.shape
    fresh = state is None
    assert mod_row0 % bb == 0
    cos, sin = _rope_tables(t, pos0)
    if bb > 1:
        cos, sin = jnp.tile(cos, (bb, 1)), jnp.tile(sin, (bb, 1))
    coef = _rotation_coefficients(cos, sin)
    new_k, new_v, new_pool, new_conv = [], [], [], []
    made = {}
    for layer in range(DEPTH):
        li = layer // 2
        where = (layer, li, mod_row0 // bb)
        if layer % 2 == 0:
            proj, made["in_even", li] = _in_even(
                x, mods, norm_g, matmul_weights["in_even", li], coef, where, bb, tm)
            seq = proj.reshape(bk, t, D_IN_EVEN)
            u = seq[:, :, 4 * D_ATTN:4 * D_ATTN + D_POOL]
            if fresh:
                pool_prefix = jnp.zeros((bk, POOL_MAX - 1, D_POOL), F32)
                attn, k, v = _attn_prompt(proj)
                new_pool.append(u[:, -(POOL_MAX - 1):])
            else:
                win_k, win_v, pool_st, _ = state
                pool_prefix = pool_st[li]
                attn = _attn_sample(proj, win_k, win_v, li, bk, t)
                k = seq[:, :, D_ATTN:2 * D_ATTN].reshape(bk, t, N_HEADS, HEAD_DIM)
                v = seq[:, :, 2 * D_ATTN:3 * D_ATTN].reshape(bk, t, N_HEADS, HEAD_DIM)
                new_pool.append(
                    jnp.concatenate([pool_prefix, u], axis=1)[:, -(POOL_MAX - 1):])
            new_k.append(k)
            new_v.append(v)
            prefix = jnp.pad(pool_prefix, ((0, 0), (1, 0), (0, 0)))
            pooled = _pool(proj, prefix, pool_w, pool_scale, li, t,
                           min(t, 1024), pos0, BF16 if fresh else F32)
            x, made["out_even", li] = _out_proj(
                attn, 0, pooled, 0, matmul_weights["out_even", li], x, mods, final_g, where,
                bb, tm_out)
        else:
            if fresh:
                prefix = jnp.zeros((bk, SUBLANES, D_CONV), F32)
            else:
                prefix = jnp.pad(state[3][li], ((0, 0), (SUBLANES - (CONV_WIDTH - 1), 0), (0, 0)))
            y, tail, made["in_odd", li] = _in_odd(
                x, mods, norm_g, matmul_weights["in_odd", li], conv_w, prefix, where, bb, tm)
            new_conv.append(tail[:, SUBLANES - (CONV_WIDTH - 1):])
            x, made["out_odd", li] = _out_proj(
                y, 0, y, 1, matmul_weights["out_odd", li], x, mods, final_g, where, bb, tm_out)
    outs = (x, jnp.stack(new_k), jnp.stack(new_v), jnp.stack(new_pool), jnp.stack(new_conv))
    return outs, made


def kernel(x_prompt, x_sample, c_prompt, c_sample, state_win_k, state_win_v, state_pool,
           state_conv, ada_w, ada_b, norm_g, w_in_even, w_out_even, pool_w, pool_scale,
           w_in_odd, conv_w, w_out_odd, final_g):
    nb, t_p, _ = x_prompt.shape
    ns, t_s, _ = x_sample.shape
    n_c = nb + ns
    c_rows = -(-n_c // SUBLANES) * SUBLANES
    c_all = jnp.pad(jnp.concatenate([c_sample, c_prompt], axis=0), ((0, c_rows - n_c), (0, 0)))
    mods = _ada_modulation(c_all, ada_w, ada_b)[:, :, None, :]
    weights = (norm_g[:, None, :], pool_w, pool_scale[:, None, :], conv_w, final_g[None, :])
    stacks = {"in_even": w_in_even, "out_even": w_out_even, "in_odd": w_in_odd,
              "out_odd": w_out_odd}
    f32_weights = {(kind, li): w for kind, w in stacks.items() for li in range(w.shape[0])}

    state = (state_win_k, state_win_v, state_pool, state_conv)
    (y_s, ksm, vsm, pools, convs), bf16_weights = _trunk(
        x_sample, mods, 0, PAST_LEN, state, weights, f32_weights, ns, t_s, t_s)
    (y_p, kp, vp, poolp, convp), _ = _trunk(
        x_prompt, mods, ns, 0, None, weights, bf16_weights, 1, 1024, 512)
    return (y_p, y_s, kp, vp, poolp, convp, ksm, vsm, pools, convs)
```

```python
import functools
import math

import jax
import jax.numpy as jnp
from jax import lax
from jax.experimental import pallas as pl
from jax.experimental.pallas import tpu as pltpu

F32 = jnp.float32
BF16 = jnp.bfloat16

D_MODEL = 2048
DEPTH = 4
PAST_LEN = 16384
D_ATTN = D_MODEL // 2
HEAD_DIM = 128
N_HEADS = D_ATTN // HEAD_DIM
PATTERNS = ((128, 1), (512, 4), (2048, 16))
KEYS_PER_PATTERN = 128
ATTN_UNROLL = 8
SAMPLE_GROUP = max(d for _, d in PATTERNS)
MATMUL_ROW_CHUNK = 128
WINDOW_MAX = max(w for w, _ in PATTERNS)
ROPE_THETA = 10000.0
D_POOL = D_MODEL // 2
POOL_WINDOWS = (2, 4, 8, 16)
POOL_GROUP = D_POOL // len(POOL_WINDOWS)
POOL_MAX = max(POOL_WINDOWS)
D_CONV = D_MODEL
CONV_WIDTH = 3
D_IN_EVEN = 4 * D_ATTN + 2 * D_POOL
RMS_EPS = 1e-6
NEG_INF = -1e30
QK_SCALE = HEAD_DIM ** -0.5 * math.log2(math.e)

LANES = 128
SUBLANES = 8
VMEM_LIMIT = 48 * 1024 * 1024


def _params(*semantics):
    return pltpu.CompilerParams(dimension_semantics=semantics,
                                vmem_limit_bytes=VMEM_LIMIT)


def _silu(v):
    return v / (1.0 + jnp.exp(-v))


def _dot(a, b):
    return jnp.dot(a, b, preferred_element_type=F32)


def _dot_nt(a, b):
    return lax.dot_general(a, b, (((1,), (1,)), ((), ())), preferred_element_type=F32)


def _rope_kernel(inv_ref, cos_ref, sin_ref, *, pos0):
    t = cos_ref.shape[0]
    pos = (lax.broadcasted_iota(jnp.int32, (t, HEAD_DIM), 0) + pos0).astype(F32)
    lane = lax.broadcasted_iota(jnp.int32, (t, HEAD_DIM), 1)
    ang = pos * inv_ref[...]
    s = jnp.sin(ang)
    cos_ref[...] = jnp.cos(ang)
    sin_ref[...] = jnp.where(lane < HEAD_DIM // 2, -s, s)


def _rope_tables(t, pos0):
    half = HEAD_DIM // 2
    inv = jnp.power(ROPE_THETA, -jnp.arange(half, dtype=F32) / half)
    inv = jnp.concatenate([inv, inv])[None, :]
    return pl.pallas_call(
        functools.partial(_rope_kernel, pos0=pos0),
        out_shape=(jax.ShapeDtypeStruct((t, HEAD_DIM), F32),
                   jax.ShapeDtypeStruct((t, HEAD_DIM), F32)),
        name="rope_tables",
    )(inv)


def _ada_kernel(c_ref, w_ref, b_ref, o_ref):
    a = _silu(c_ref[...]).astype(BF16)
    o_ref[...] = _dot(a, w_ref[...].astype(BF16)) + b_ref[...]


def _ada_modulation(c_all, ada_w, ada_b):
    rows = c_all.shape[0]
    n_out = ada_w.shape[2]
    tn = 1024
    return pl.pallas_call(
        _ada_kernel,
        grid=(DEPTH, n_out // tn),
        in_specs=[
            pl.BlockSpec((rows, D_MODEL), lambda l, n: (0, 0)),
            pl.BlockSpec((None, D_MODEL, tn), lambda l, n: (l, 0, n)),
            pl.BlockSpec((None, 1, tn), lambda l, n: (l, 0, n)),
        ],
        out_specs=pl.BlockSpec((None, rows, tn), lambda l, n: (l, 0, n)),
        out_shape=jax.ShapeDtypeStruct((DEPTH, rows, n_out), F32),
        compiler_params=_params("arbitrary", "arbitrary"),
        name="ada_modulation",
    )(c_all, ada_w, ada_b.reshape(DEPTH, 1, n_out))


def _row_chunks(bb, tm):
    chunk = min(tm, MATMUL_ROW_CHUNK) if bb == 1 else bb * tm
    return [slice(c * chunk, (c + 1) * chunk) for c in range(bb * tm // chunk)]


def _normalized_rows(first, rs, bb, x_ref, shift_ref, scale_ref, g_ref, h_ref):
    if not first:
        return h_ref[rs, :]
    h = _modulated_norm(x_ref, shift_ref, scale_ref, g_ref, rs if bb == 1 else None)
    h_ref[rs, :] = h
    return h


def _modulated_norm(x_ref, shift_ref, scale_ref, g_ref, rows=None):
    x = x_ref[...] if rows is None else x_ref[:, rows, :]
    ms = jnp.mean(x * x, axis=-1, keepdims=True)
    y = x * lax.rsqrt(ms + RMS_EPS) * g_ref[...]
    h = y * (1.0 + scale_ref[...]) + shift_ref[...]
    bb, tm, d = h.shape
    return h.reshape(bb * tm, d).astype(BF16)


def _bf16_weights(w_ref, emit_ref):
    if emit_ref is None:
        return w_ref[...]
    w = w_ref[...].astype(BF16)
    emit_ref[...] = w
    return w


def _in_even_kernel(x_ref, shift_ref, scale_ref, g_ref, w_ref, cos_ref, sin_ref,
                    o_ref, *rest):
    emit_ref, h_ref = rest if len(rest) == 2 else (None, rest[0])
    n = pl.program_id(2)
    tn = w_ref.shape[1]
    per = D_ATTN // tn
    assert D_POOL == D_ATTN and D_ATTN % tn == 0

    is_gate = ((n >= 3 * per) & (n < 4 * per)) | (n >= 5 * per)
    bb, tm, _ = x_ref.shape

    def step(first):
        for rs in _row_chunks(bb, tm):
            h = _normalized_rows(first, rs, bb, x_ref, shift_ref, scale_ref, g_ref, h_ref)
            acc = _dot(h, _bf16_weights(w_ref, emit_ref))
            cos = cos_ref[rs, :]
            sin = sin_ref[rs, :]
            for j in range(tn // HEAD_DIM):
                cs = slice(j * HEAD_DIM, (j + 1) * HEAD_DIM)
                a = acc[:, cs]
                lin = a * cos + pltpu.roll(a, HEAD_DIM // 2, axis=1) * sin
                o_ref[j, rs, :] = jnp.where(is_gate, _silu(lin), lin)

    pl.when(n == 0)(functools.partial(step, True))
    pl.when(n > 0)(functools.partial(step, False))


def _mod_spec(where, bb, width, col_blk):
    layer, _, mod_blk0 = where
    return pl.BlockSpec((None, bb, 1, width),
                        lambda b, i, n: (layer, mod_blk0 + b, 0, col_blk(n)))


def _rotation_coefficients(cos_rows, sin_rows):
    ones, zeros = jnp.ones_like(cos_rows), jnp.zeros_like(sin_rows)
    return jnp.stack([jnp.stack([cos_rows * QK_SCALE, sin_rows * QK_SCALE]),
                      jnp.stack([cos_rows, sin_rows]),
                      jnp.stack([ones, zeros])])


def _weight_specs(w, li, rows, cols, col_blk):
    flat = lambda b, i, n: (0, col_blk(n))
    if w.ndim == 2:
        return pl.BlockSpec((rows, cols), flat), None, None
    assert w.dtype == F32 and w.shape[1] == rows
    return (pl.BlockSpec((None, rows, cols), lambda b, i, n: (li, 0, col_blk(n))),
            pl.BlockSpec((rows, cols), flat), jax.ShapeDtypeStruct((rows, w.shape[2]), BF16))


def _in_even(x, mods, norm_g, w, coef, where, bb, tm):
    layer, li, _ = where
    bk, t, d = x.shape
    n_out = w.shape[-1]
    tn = 512
    per = D_ATTN // tn
    rows = bb * tm
    grid = (bk // bb, t // tm, n_out // tn)
    coef_spec = lambda which: pl.BlockSpec(
        (None, None, rows, HEAD_DIM),
        lambda b, i, n: (jnp.minimum(n // per, 2), which, i, 0))
    w_spec, emit_spec, emit_shape = _weight_specs(w, li, d, tn, lambda n: n)
    emit = emit_spec is not None
    assert not emit or grid[:2] == (1, 1)
    slabs = tn // HEAD_DIM
    out_specs = [pl.BlockSpec((None, slabs, rows, HEAD_DIM), lambda b, i, n: (b, n, i, 0))]
    out_shape = [jax.ShapeDtypeStruct(
        (bk // bb, n_out // HEAD_DIM, (t // tm) * rows, HEAD_DIM), F32)]
    outs = pl.pallas_call(
        _in_even_kernel,
        grid=grid,
        in_specs=[
            pl.BlockSpec((bb, tm, d), lambda b, i, n: (b, i, 0)),
            _mod_spec(where, bb, d, lambda n: 0),
            _mod_spec(where, bb, d, lambda n: 1),
            pl.BlockSpec((None, 1, d), lambda b, i, n: (layer, 0, 0)),
            w_spec,
            coef_spec(0),
            coef_spec(1),
        ],
        out_specs=out_specs + ([emit_spec] if emit else []),
        out_shape=out_shape + ([emit_shape] if emit else []),
        scratch_shapes=[pltpu.VMEM((rows, d), BF16)],
        compiler_params=_params("arbitrary", "arbitrary", "arbitrary"),
        name="in_proj_even",
    )(x, mods, mods, norm_g, w, coef, coef)
    return outs[0], (outs[1] if emit else None)


def _in_odd_kernel(x_ref, shift_ref, scale_ref, g_ref, wb_ref, wc_ref, wh_ref, wg_ref,
                   cw_ref, pre_ref, o_ref, tail_ref, *rest):
    *emit_refs, h_ref, zbuf_ref = rest
    eb_ref, ec_ref, eh_ref, eg_ref = emit_refs if emit_refs else (None,) * 4
    i = pl.program_id(1)
    n = pl.program_id(2)
    bb, tm, _ = x_ref.shape
    tn = wb_ref.shape[1]
    halo = SUBLANES
    tail_cols = pl.ds(pl.multiple_of(n * tn, tn), tn)

    @pl.when((n == 0) & (i == 0))
    def _():
        tail_ref[...] = pre_ref[...]

    def step(first):
        zbuf_ref[:, 0:halo, :] = tail_ref[:, :, tail_cols]
        cw = cw_ref[...]
        for rs in _row_chunks(bb, tm):
            lo, hi = (rs.start, rs.stop) if bb == 1 else (0, tm)
            h = _normalized_rows(first, rs, bb, x_ref, shift_ref, scale_ref, g_ref, h_ref)
            z = (_dot(h, _bf16_weights(wc_ref, ec_ref))
                 * _dot(h, _bf16_weights(wh_ref, eh_ref))).reshape(bb, hi - lo, tn)
            zbuf_ref[:, halo + lo:halo + hi, :] = z
            yc = zbuf_ref[:, halo - 2 + lo:halo - 2 + hi, :] * cw[0:1]
            yc = yc + zbuf_ref[:, halo - 1 + lo:halo - 1 + hi, :] * cw[1:2]
            yc = yc + z * cw[2:3]
            bg = _dot(h, _bf16_weights(wb_ref, eb_ref)).reshape(bb, hi - lo, tn)
            gate = _dot(h, _bf16_weights(wg_ref, eg_ref)).reshape(bb, hi - lo, tn)
            y = bg * yc * _silu(gate)
            o_ref[rs, :] = y.reshape(rs.stop - rs.start, tn).astype(BF16)
        tail_ref[:, :, tail_cols] = zbuf_ref[:, tm:tm + halo, :]

    pl.when(n == 0)(functools.partial(step, True))
    pl.when(n > 0)(functools.partial(step, False))


def _in_odd(x, mods, norm_g, w, conv_w, prefix, where, bb, tm):
    layer, li, _ = where
    bk, t, d = x.shape
    tn = 256
    nt = D_CONV // tn
    rows = bb * tm
    grid = (bk // bb, t // tm, nt)
    emit = not isinstance(w, (tuple, list))
    assert not emit or grid[:2] == (1, 1)
    if emit:
        specs = [_weight_specs(w, li, d, tn, lambda n, k=k: k * nt + n) for k in range(4)]
        emit_specs = [pl.BlockSpec((d, tn), lambda b, i, n: (0, n))] * 4
        emit_shapes = [jax.ShapeDtypeStruct((d, D_CONV), BF16)] * 4
        w_args = (w,) * 4
    else:
        specs = [_weight_specs(wk, li, d, tn, lambda n: n) for wk in w]
        emit_specs, emit_shapes = [], []
        w_args = tuple(w)
    outs = pl.pallas_call(
        _in_odd_kernel,
        grid=grid,
        in_specs=[
            pl.BlockSpec((bb, tm, d), lambda b, i, n: (b, i, 0)),
            _mod_spec(where, bb, d, lambda n: 0),
            _mod_spec(where, bb, d, lambda n: 1),
            pl.BlockSpec((None, 1, d), lambda b, i, n: (layer, 0, 0)),
            *[s[0] for s in specs],
            pl.BlockSpec((None, CONV_WIDTH, tn), lambda b, i, n: (li, 0, n)),
            pl.BlockSpec((bb, SUBLANES, D_CONV), lambda b, i, n: (b, 0, 0)),
        ],
        out_specs=[
            pl.BlockSpec((None, rows, tn), lambda b, i, n: (b, i, n)),
            pl.BlockSpec((bb, SUBLANES, D_CONV), lambda b, i, n: (b, 0, 0)),
        ] + emit_specs,
        out_shape=[
            jax.ShapeDtypeStruct((bk // bb, (t // tm) * rows, D_CONV), BF16),
            jax.ShapeDtypeStruct((bk, SUBLANES, D_CONV), F32),
        ] + emit_shapes,
        scratch_shapes=[
            pltpu.VMEM((rows, d), BF16),
            pltpu.VMEM((bb, SUBLANES + tm, tn), F32),
        ],
        compiler_params=_params("arbitrary", "arbitrary", "arbitrary"),
        name="in_proj_odd_conv",
    )(x, mods, mods, norm_g, *w_args, conv_w, prefix)
    return outs[0], outs[1], (tuple(outs[2:]) if emit else None)


def _out_proj_kernel(a1_ref, a2_ref, w1_ref, w2_ref, x_ref, gate_ref, fg_ref, o_ref,
                     *emit_refs, final):
    e1_ref, e2_ref = emit_refs if emit_refs else (None, None)
    bb, tm, d = x_ref.shape
    for rs in _row_chunks(bb, tm):
        lo, hi = (rs.start, rs.stop) if bb == 1 else (0, tm)
        acc = _dot(a1_ref[rs, :].astype(BF16), _bf16_weights(w1_ref, e1_ref))
        acc = acc + _dot(a2_ref[rs, :].astype(BF16), _bf16_weights(w2_ref, e2_ref))
        x = x_ref[:, lo:hi, :] + gate_ref[...] * acc.reshape(bb, hi - lo, d)
        if final:
            ms = jnp.mean(x * x, axis=-1, keepdims=True)
            x = x * lax.rsqrt(ms + RMS_EPS) * fg_ref[...]
        o_ref[:, lo:hi, :] = x


def _out_proj(a1, a1_blk, a2, a2_blk, w, x, mods, final_g, where, bb, tm):
    layer, li, mod_blk0 = where
    bk, t, d = x.shape
    rows = bb * tm
    grid = (bk // bb, t // tm)
    emit = not isinstance(w, (tuple, list))
    assert not emit or grid == (1, 1)
    if emit:
        kh = w.shape[1] // 2
        w_specs = [pl.BlockSpec((None, kh, d), lambda b, i, k=k: (li, k, 0)) for k in range(2)]
        emit_specs = [pl.BlockSpec((kh, d), lambda b, i: (0, 0))] * 2
        emit_shapes = [jax.ShapeDtypeStruct((kh, d), BF16)] * 2
        w_args = (w, w)
    else:
        kh = w[0].shape[0]
        w_specs = [pl.BlockSpec((kh, d), lambda b, i: (0, 0))] * 2
        emit_specs, emit_shapes = [], []
        w_args = tuple(w)
    outs = pl.pallas_call(
        functools.partial(_out_proj_kernel, final=layer == DEPTH - 1),
        grid=grid,
        in_specs=[
            pl.BlockSpec((None, rows, kh), lambda b, i: (b, i, a1_blk)),
            pl.BlockSpec((None, rows, kh), lambda b, i: (b, i, a2_blk)),
            *w_specs,
            pl.BlockSpec((bb, tm, d), lambda b, i: (b, i, 0)),
            pl.BlockSpec((None, bb, 1, d), lambda b, i: (layer, mod_blk0 + b, 0, 2)),
            pl.BlockSpec((1, d), lambda b, i: (0, 0)),
        ],
        out_specs=[pl.BlockSpec((bb, tm, d), lambda b, i: (b, i, 0))] + emit_specs,
        out_shape=[jax.ShapeDtypeStruct((bk, t, d), F32)] + emit_shapes,
        compiler_params=_params("arbitrary", "arbitrary"),
        name="out_proj_residual",
    )(a1, a2, *w_args, x, mods, final_g)
    return outs[0], (tuple(outs[1:]) if emit else None)


def _attn_prompt_kernel(q_ref, k_ref, v_ref, ga_ref, bias_ref, o_ref, kt_ref, vt_ref,
                        acc_ref, m_ref, l_ref):
    t = q_ref.shape[0]
    bq = KEYS_PER_PATTERN
    dilations = sorted((d for _, d in PATTERNS), reverse=True)
    assert dilations[-1] == 1

    keep, nh, _ = kt_ref.shape
    head = pl.program_id(1)
    for src_ref, dst_ref in ((k_ref, kt_ref), (v_ref, vt_ref)):
        dst = dst_ref.reshape(keep * nh, HEAD_DIM)
        dst[pl.ds(head, keep, stride=nh), :] = src_ref[pl.ds(t - keep, keep), :]

    for pi, d in enumerate(dilations):
        span = bq * d
        nblk = t // span
        first = pi == 0
        last = pi == len(dilations) - 1

        def rows(ref, start, n, d=d):
            if d == 1:
                return ref[pl.ds(pl.multiple_of(start, bq), n), :]
            return ref[pl.ds(start, n, stride=d), :]

        def put(ref, start, val, d=d):
            if d == 1:
                ref[pl.ds(pl.multiple_of(start, bq), bq), :] = val
            else:
                ref[pl.ds(start, bq, stride=d), :] = val

        def body(it, carry, nblk=nblk, span=span, rows=rows, put=put, first=first, last=last):
            bases, kbases, has_prevs = [], [], []
            for u in range(ATTN_UNROLL):
                itu = it * ATTN_UNROLL + u
                blk = itu % nblk
                base = itu // nblk + blk * span
                has_prev = jnp.minimum(blk, 1)
                bases.append(base)
                kbases.append(base - has_prev * span)
                has_prevs.append(has_prev)
            stack = lambda ref, starts, n: jnp.stack([rows(ref, s, n) for s in starts])

            def keys(ref):
                if nblk == 2 and ATTN_UNROLL % 2 == 0:
                    pairs = [rows(ref, s, 2 * bq).astype(BF16) for s in kbases[::2]]
                    return jnp.stack([pairs[u // 2] for u in range(ATTN_UNROLL)])
                return stack(ref, kbases, 2 * bq).astype(BF16)

            q = stack(q_ref, bases, bq).astype(BF16)
            k = keys(k_ref)
            v = keys(v_ref)
            s = jnp.einsum('uqd,ukd->uqk', q, k, preferred_element_type=F32)
            s = s + jnp.stack([bias_ref[hp] for hp in has_prevs])
            m = jnp.max(jnp.maximum(s[:, :, :bq], s[:, :, bq:]), axis=-1, keepdims=True)
            p = jnp.exp2(s - m)
            l = jnp.sum(p[:, :, :bq] + p[:, :, bq:], axis=-1, keepdims=True)
            num = jnp.einsum('uqk,ukd->uqd', p.astype(BF16), v, preferred_element_type=F32)
            m = jnp.broadcast_to(m, num.shape)
            l = jnp.broadcast_to(l, num.shape)
            if not first:
                m_old = stack(m_ref, bases, bq)
                m_new = jnp.maximum(m_old, m)
                a_old = jnp.exp2(m_old - m_new)
                a_new = jnp.exp2(m - m_new)
                num = a_old * stack(acc_ref, bases, bq) + a_new * num
                l = a_old * stack(l_ref, bases, bq) + a_new * l
                m = m_new
            for u, base in enumerate(bases):
                if last:
                    rows_out = pl.ds(pl.multiple_of(base, bq), bq)
                    o_ref[rows_out, :] = (num[u] / l[u] * ga_ref[rows_out, :]).astype(BF16)
                else:
                    put(acc_ref, base, num[u])
                    put(m_ref, base, m[u])
                    put(l_ref, base, l[u])
            return carry

        lax.fori_loop(0, t // (bq * ATTN_UNROLL), body, 0, unroll=2)


def _attn_prompt(proj):
    b, _, t, _ = proj.shape
    for w, d in PATTERNS:
        assert w // d == KEYS_PER_PATTERN and t % (2 * KEYS_PER_PATTERN * d) == 0
    assert (t // KEYS_PER_PATTERN) % ATTN_UNROLL == 0
    col = lambda off: pl.BlockSpec((None, None, t, HEAD_DIM),
                                   lambda bi, h, off=off: (bi, off + h, 0, 0))
    bq = KEYS_PER_PATTERN
    qi = jnp.arange(bq, dtype=jnp.int32)[:, None]
    kj = jnp.arange(2 * bq, dtype=jnp.int32)[None, :]
    visible = jnp.stack([kj <= qi, (kj >= qi) & (kj <= qi + bq)])
    bias = jnp.where(visible, 0.0, NEG_INF).astype(F32)
    keep = min(WINDOW_MAX, t)
    tail = pl.BlockSpec((None, keep, N_HEADS, HEAD_DIM), lambda bi, h: (bi, 0, 0, 0),
                        pipeline_mode=pl.Buffered(1))
    tail_shape = jax.ShapeDtypeStruct((b, keep, N_HEADS, HEAD_DIM), F32)
    return pl.pallas_call(
        _attn_prompt_kernel,
        grid=(b, N_HEADS),
        in_specs=[col(0), col(N_HEADS), col(2 * N_HEADS), col(3 * N_HEADS),
                  pl.BlockSpec((2, bq, 2 * bq), lambda bi, h: (0, 0, 0))],
        out_specs=(pl.BlockSpec((None, t, HEAD_DIM), lambda bi, h: (bi, 0, h)), tail, tail),
        out_shape=(jax.ShapeDtypeStruct((b, t, D_ATTN), BF16), tail_shape, tail_shape),
        scratch_shapes=[pltpu.VMEM((t, HEAD_DIM), F32)] * 3,
        compiler_params=_params("arbitrary", "arbitrary"),
        name="attn_prompt",
    )(proj, proj, proj, proj, bias)


def _pattern_count(delta):
    cnt = jnp.zeros(delta.shape, F32)
    for w, d in PATTERNS:
        assert d & (d - 1) == 0
        ok = (delta >= 0) & (delta <= w) & (jnp.bitwise_and(delta, d - 1) == 0)
        cnt = cnt + ok.astype(F32)
    return cnt


def _attn_sample_kernel(q_ref, kn_ref, vn_ref, ga_ref, ka_ref, kb_ref, va_ref, vb_ref, o_ref,
                        *, wb):
    nh, tq, _ = q_ref.shape
    assert tq & (tq - 1) == 0 and nh & (nh - 1) == 0 and tq == SUBLANES
    tq_bits, nh_bits, sub_bits = (n.bit_length() - 1 for n in (tq, nh, SUBLANES))
    nq = nh * tq

    def heads_to_rows(ref):
        return ref[...].reshape(nq, HEAD_DIM)

    def flat(ref):
        g, j, h, d = ref.shape
        return ref[...].reshape(g * j * h, d).astype(BF16)

    def window_weights(n_groups, group0, residue0):
        shape = (nq, n_groups * SUBLANES * nh)
        r = lax.broadcasted_iota(jnp.int32, shape, 0)
        c = lax.broadcasted_iota(jnp.int32, shape, 1)
        key_head = c & (nh - 1)
        residue = residue0 + ((c >> nh_bits) & (SUBLANES - 1))
        key_row = (group0 + (c >> (nh_bits + sub_bits))) * SAMPLE_GROUP + residue
        delta = wb + (r & (tq - 1)) - key_row
        return jnp.where((r >> tq_bits) == key_head, _pattern_count(delta), 0.0)

    r = lax.broadcasted_iota(jnp.int32, (nq, nq), 0)
    c = lax.broadcasted_iota(jnp.int32, (nq, nq), 1)
    w_new = jnp.where((r >> tq_bits) == (c >> tq_bits),
                      _pattern_count((r & (tq - 1)) - (c & (tq - 1))), 0.0)
    n_dense = kb_ref.shape[0]
    parts = (
        (flat(ka_ref), flat(va_ref), window_weights(ka_ref.shape[0], 0, 0)),
        (flat(kb_ref), flat(vb_ref),
         window_weights(n_dense, wb // SAMPLE_GROUP - n_dense, SUBLANES)),
        (heads_to_rows(kn_ref).astype(BF16), heads_to_rows(vn_ref).astype(BF16), w_new),
    )
    q = heads_to_rows(q_ref).astype(BF16)
    scores = [jnp.where(wgt > 0, _dot_nt(q, k), NEG_INF) for k, _, wgt in parts]
    m = functools.reduce(jnp.maximum, [jnp.max(s, axis=-1, keepdims=True) for s in scores])
    den = jnp.zeros((nq, 1), F32)
    num = jnp.zeros((nq, HEAD_DIM), F32)
    for s, (_, v, wgt) in zip(scores, parts):
        p = wgt * jnp.exp2(s - m)
        den = den + jnp.sum(p, axis=-1, keepdims=True)
        num = num + _dot(p.astype(BF16), v)
    out = num / den
    for h in range(nh):
        cs = slice(h * HEAD_DIM, (h + 1) * HEAD_DIM)
        o_ref[:, cs] = out[h * tq:(h + 1) * tq] * ga_ref[h]


def _attn_sample(proj, win_k, win_v, li, n_seq, tq):
    wb = win_k.shape[2]
    dense_reach = max(w for w, d in PATTERNS if d < SAMPLE_GROUP)
    assert max(d for _, d in PATTERNS) == SAMPLE_GROUP == 2 * SUBLANES and tq == SUBLANES
    assert wb % SAMPLE_GROUP == 0 and dense_reach % SAMPLE_GROUP == 0 and wb % dense_reach == 0
    n_groups = wb // SAMPLE_GROUP
    n_dense = dense_reach // SAMPLE_GROUP
    win_k = win_k.reshape(-1, n_seq, n_groups, SAMPLE_GROUP, N_HEADS, HEAD_DIM)
    win_v = win_v.reshape(-1, n_seq, n_groups, SAMPLE_GROUP, N_HEADS, HEAD_DIM)
    col = lambda blk: pl.BlockSpec((None, N_HEADS, tq, HEAD_DIM), lambda s, blk=blk: (0, blk, s, 0))
    sparse = pl.BlockSpec((None, None, n_groups, SUBLANES, N_HEADS, HEAD_DIM),
                          lambda s: (li, s, 0, 0, 0, 0))
    dense = pl.BlockSpec((None, None, n_dense, SUBLANES, N_HEADS, HEAD_DIM),
                         lambda s: (li, s, n_groups // n_dense - 1, 1, 0, 0))
    return pl.pallas_call(
        functools.partial(_attn_sample_kernel, wb=wb),
        grid=(n_seq,),
        in_specs=[col(0), col(1), col(2), col(3), sparse, dense, sparse, dense],
        out_specs=pl.BlockSpec((None, tq, D_ATTN), lambda s: (0, s, 0)),
        out_shape=jax.ShapeDtypeStruct((1, n_seq * tq, D_ATTN), F32),
        compiler_params=_params("arbitrary"),
        name="attn_sample",
    )(proj, proj, proj, proj, win_k, win_k, win_v, win_v)


def _pool_kernel(u_ref, halo_ref, pre_ref, gp_ref, w_ref, s_ref, o_ref, ext_ref, tmp_ref,
                 *, pos0):
    i = pl.program_id(1)
    g = pl.program_id(2)
    n_slab, tm, _ = u_ref.shape
    pad = SUBLANES
    first = pad + POOL_MAX
    n_ext = first + tm
    assert POOL_MAX // 2 <= pad and n_ext == ext_ref.shape[1]
    lanes = lambda slabs: jnp.concatenate([slabs[s] for s in range(n_slab)], axis=-1)

    zeros = jnp.zeros((n_slab, pad, HEAD_DIM), F32)
    ext_ref[:, 0:pad, :] = zeros
    tmp_ref[:, 0:pad, :] = zeros

    @pl.when(i == 0)
    def _():
        for s in range(n_slab):
            ext_ref[s, pad:first, :] = pre_ref[:, s * HEAD_DIM:(s + 1) * HEAD_DIM]

    @pl.when(i > 0)
    def _():
        ext_ref[:, pad:first, :] = halo_ref[...]

    u = u_ref[...]
    ext_ref[:, first:, :] = u
    pos = pos0 + i * tm + lax.broadcasted_iota(jnp.int32, (1, tm, 1), 1)

    for gi, w in enumerate(POOL_WINDOWS):
        assert w & (w - 1) == 0

        @pl.when(g == gi)
        def _(w=w):
            src, dst = ext_ref, tmp_ref
            span = 1
            while span < w:
                dst[:, pad:n_ext, :] = (src[:, pad:n_ext, :]
                                        + src[:, pad - span:n_ext - span, :])
                src, dst = dst, src
                span *= 2
            win = src[:, first:n_ext, :]
            inv_cnt = 1.0 / jnp.minimum(w, pos + 1).astype(F32)
            pooled = lanes(win * inv_cnt - u)
            mixed = _dot(pooled.astype(BF16), w_ref[...].astype(BF16)) * s_ref[...]
            o_ref[...] = (mixed * lanes(gp_ref[...])).astype(o_ref.dtype)


def _pool(proj, prefix, pool_w, pool_scale, li, seq_rows, tm, pos0, out_dtype):
    g_, _, r_, _ = proj.shape
    n_seq = g_ * (r_ // seq_rows)
    nt = seq_rows // tm
    per_g = r_ // seq_rows
    ng = len(POOL_WINDOWS)
    n_slab = POOL_GROUP // HEAD_DIM
    u_blk0 = 4 * D_ATTN // POOL_GROUP
    gp_blk0 = u_blk0 + ng
    hb = tm // POOL_MAX

    def slab_map(blk0):
        return lambda s, i, g: (s // per_g, blk0 + g, (s % per_g) * nt + i, 0)

    def halo_map(s, i, g):
        return (s // per_g, u_blk0 + g, jnp.maximum(((s % per_g) * nt + i) * hb - 1, 0), 0)

    return pl.pallas_call(
        functools.partial(_pool_kernel, pos0=pos0),
        grid=(n_seq, nt, ng),
        in_specs=[
            pl.BlockSpec((None, n_slab, tm, HEAD_DIM), slab_map(u_blk0)),
            pl.BlockSpec((None, n_slab, POOL_MAX, HEAD_DIM), halo_map),
            pl.BlockSpec((None, POOL_MAX, POOL_GROUP), lambda s, i, g: (s, 0, g)),
            pl.BlockSpec((None, n_slab, tm, HEAD_DIM), slab_map(gp_blk0)),
            pl.BlockSpec((None, None, POOL_GROUP, POOL_GROUP), lambda s, i, g: (li, g, 0, 0)),
            pl.BlockSpec((None, 1, POOL_GROUP), lambda s, i, g: (li, 0, g)),
        ],
        out_specs=pl.BlockSpec(
            (None, tm, POOL_GROUP), lambda s, i, g: (s // per_g, (s % per_g) * nt + i, g)),
        out_shape=jax.ShapeDtypeStruct((g_, r_, D_POOL), out_dtype),
        scratch_shapes=[pltpu.VMEM((n_slab, SUBLANES + POOL_MAX + tm, HEAD_DIM), F32)] * 2,
        compiler_params=_params("arbitrary", "arbitrary", "arbitrary"),
        name="pool_mixer",
    )(proj, proj, prefix, proj, pool_w, pool_scale)


def _trunk(x, mods, mod_row0, pos0, state, weights, matmul_weights, bb, tm, tm_out):
    norm_g, pool_w, pool_scale, conv_w, final_g = weights
    bk, t, _ = x.shape
    fresh = state is None
    assert mod_row0 % bb == 0
    cos, sin = _rope_tables(t, pos0)
    if bb > 1:
        cos, sin = jnp.tile(cos, (bb, 1)), jnp.tile(sin, (bb, 1))
    coef = _rotation_coefficients(cos, sin)
    new_k, new_v, new_pool, new_conv = [], [], [], []
    made = {}
    for layer in range(DEPTH):
        li = layer // 2
        where = (layer, li, mod_row0 // bb)
        if layer % 2 == 0:
            proj, made["in_even", li] = _in_even(
                x, mods, norm_g, matmul_weights["in_even", li], coef, where, bb, tm)

            def seq_rows(first_slab, n_last, proj=proj):
                g = proj.shape[0]
                s = proj[:, first_slab:first_slab + N_HEADS]
                s = s.reshape(g, N_HEADS, bk // g, t, HEAD_DIM)[:, :, :, t - n_last:]
                return jnp.transpose(s, (0, 2, 3, 1, 4)).reshape(bk, n_last, N_HEADS, HEAD_DIM)

            if fresh:
                pool_prefix = jnp.zeros((bk, POOL_MAX - 1, D_POOL), F32)
                attn, k, v = _attn_prompt(proj)
                u_tail = seq_rows(4 * N_HEADS, POOL_MAX - 1).reshape(bk, -1, D_POOL)
            else:
                win_k, win_v, pool_st, _ = state
                pool_prefix = pool_st[li]
                attn = _attn_sample(proj, win_k, win_v, li, bk, t)
                k = seq_rows(N_HEADS, t)
                v = seq_rows(2 * N_HEADS, t)
                u_tail = jnp.concatenate(
                    [pool_prefix, seq_rows(4 * N_HEADS, t).reshape(bk, t, D_POOL)],
                    axis=1)[:, -(POOL_MAX - 1):]
            new_pool.append(u_tail)
            new_k.append(k)
            new_v.append(v)
            prefix = jnp.pad(pool_prefix, ((0, 0), (1, 0), (0, 0)))
            pooled = _pool(proj, prefix, pool_w, pool_scale, li, t,
                           min(t, 1024), pos0, BF16 if fresh else F32)
            x, made["out_even", li] = _out_proj(
                attn, 0, pooled, 0, matmul_weights["out_even", li], x, mods, final_g, where,
                bb, tm_out)
        else:
            if fresh:
                prefix = jnp.zeros((bk, SUBLANES, D_CONV), F32)
            else:
                prefix = jnp.pad(state[3][li], ((0, 0), (SUBLANES - (CONV_WIDTH - 1), 0), (0, 0)))
            y, tail, made["in_odd", li] = _in_odd(
                x, mods, norm_g, matmul_weights["in_odd", li], conv_w, prefix, where, bb, tm)
            new_conv.append(tail[:, SUBLANES - (CONV_WIDTH - 1):])
            x, made["out_odd", li] = _out_proj(
                y, 0, y, 1, matmul_weights["out_odd", li], x, mods, final_g, where, bb, tm_out)
    outs = (x, jnp.stack(new_k), jnp.stack(new_v), jnp.stack(new_pool), jnp.stack(new_conv))
    return outs, made


def kernel(x_prompt, x_sample, c_prompt, c_sample, state_win_k, state_win_v, state_pool,
           state_conv, ada_w, ada_b, norm_g, w_in_even, w_out_even, pool_w, pool_scale,
           w_in_odd, conv_w, w_out_odd, final_g):
    nb, t_p, _ = x_prompt.shape
    ns, t_s, _ = x_sample.shape
    n_c = nb + ns
    c_rows = -(-n_c // SUBLANES) * SUBLANES
    c_all = jnp.pad(jnp.concatenate([c_sample, c_prompt], axis=0), ((0, c_rows - n_c), (0, 0)))
    mods = _ada_modulation(c_all, ada_w, ada_b)[:, :, None, :]
    weights = (norm_g[:, None, :], pool_w, pool_scale[:, None, :], conv_w, final_g[None, :])
    stacks = {"in_even": w_in_even, "out_even": w_out_even, "in_odd": w_in_odd,
              "out_odd": w_out_odd}
    f32_weights = {(kind, li): w for kind, w in stacks.items() for li in range(w.shape[0])}

    state = (state_win_k, state_win_v, state_pool, state_conv)
    (y_s, ksm, vsm, pools, convs), bf16_weights = _trunk(
        x_sample, mods, 0, PAST_LEN, state, weights, f32_weights, ns, t_s, t_s)
    (y_p, kp, vp, poolp, convp), _ = _trunk(
        x_prompt, mods, ns, 0, None, weights, bf16_weights, 1, 1024, 512)
    return (y_p, y_s, kp, vp, poolp, convp, ksm, vsm, pools, convs)
```

```python
import functools
import math

import jax
import jax.numpy as jnp
from jax import lax
from jax.experimental import pallas as pl
from jax.experimental.pallas import tpu as pltpu

F32 = jnp.float32
BF16 = jnp.bfloat16

D_MODEL = 2048
DEPTH = 4
PAST_LEN = 16384
D_ATTN = D_MODEL // 2
HEAD_DIM = 128
N_HEADS = D_ATTN // HEAD_DIM
PATTERNS = ((128, 1), (512, 4), (2048, 16))
KEYS_PER_PATTERN = 128
ATTN_UNROLL = 8
SAMPLE_GROUP = max(d for _, d in PATTERNS)
MATMUL_ROW_CHUNK = 128
WINDOW_MAX = max(w for w, _ in PATTERNS)
ROPE_THETA = 10000.0
D_POOL = D_MODEL // 2
POOL_WINDOWS = (2, 4, 8, 16)
POOL_GROUP = D_POOL // len(POOL_WINDOWS)
POOL_MAX = max(POOL_WINDOWS)
D_CONV = D_MODEL
CONV_WIDTH = 3
D_IN_EVEN = 4 * D_ATTN + 2 * D_POOL
RMS_EPS = 1e-6
NEG_INF = -1e30
QK_SCALE = HEAD_DIM ** -0.5 * math.log2(math.e)

LANES = 128
SUBLANES = 8
VMEM_LIMIT = 48 * 1024 * 1024


def _params(*semantics):
    return pltpu.CompilerParams(dimension_semantics=semantics,
                                vmem_limit_bytes=VMEM_LIMIT)


def _silu(v):
    return v / (1.0 + jnp.exp(-v))


def _dot(a, b):
    return jnp.dot(a, b, preferred_element_type=F32)


def _dot_nt(a, b):
    return lax.dot_general(a, b, (((1,), (1,)), ((), ())), preferred_element_type=F32)


def _rope_kernel(inv_ref, cos_ref, sin_ref, *, pos0):
    t = cos_ref.shape[0]
    pos = (lax.broadcasted_iota(jnp.int32, (t, HEAD_DIM), 0) + pos0).astype(F32)
    lane = lax.broadcasted_iota(jnp.int32, (t, HEAD_DIM), 1)
    ang = pos * inv_ref[...]
    s = jnp.sin(ang)
    cos_ref[...] = jnp.cos(ang)
    sin_ref[...] = jnp.where(lane < HEAD_DIM // 2, -s, s)


def _rope_tables(t, pos0):
    half = HEAD_DIM // 2
    inv = jnp.power(ROPE_THETA, -jnp.arange(half, dtype=F32) / half)
    inv = jnp.concatenate([inv, inv])[None, :]
    return pl.pallas_call(
        functools.partial(_rope_kernel, pos0=pos0),
        out_shape=(jax.ShapeDtypeStruct((t, HEAD_DIM), F32),
                   jax.ShapeDtypeStruct((t, HEAD_DIM), F32)),
        name="rope_tables",
    )(inv)


def _ada_kernel(c_ref, w_ref, b_ref, o_ref):
    a = _silu(c_ref[...]).astype(BF16)
    o_ref[...] = _dot(a, w_ref[...].astype(BF16)) + b_ref[...]


def _ada_modulation(c_all, ada_w, ada_b):
    rows = c_all.shape[0]
    n_out = ada_w.shape[2]
    tn = 1024
    return pl.pallas_call(
        _ada_kernel,
        grid=(DEPTH, n_out // tn),
        in_specs=[
            pl.BlockSpec((rows, D_MODEL), lambda l, n: (0, 0)),
            pl.BlockSpec((None, D_MODEL, tn), lambda l, n: (l, 0, n)),
            pl.BlockSpec((None, 1, tn), lambda l, n: (l, 0, n)),
        ],
        out_specs=pl.BlockSpec((None, rows, tn), lambda l, n: (l, 0, n)),
        out_shape=jax.ShapeDtypeStruct((DEPTH, rows, n_out), F32),
        compiler_params=_params("arbitrary", "arbitrary"),
        name="ada_modulation",
    )(c_all, ada_w, ada_b.reshape(DEPTH, 1, n_out))


def _row_chunks(bb, tm):
    chunk = min(tm, MATMUL_ROW_CHUNK) if bb == 1 else bb * tm
    return [slice(c * chunk, (c + 1) * chunk) for c in range(bb * tm // chunk)]


def _normalized_rows(first, rs, bb, x_ref, shift_ref, scale_ref, g_ref, h_ref):
    if not first:
        return h_ref[rs, :]
    h = _modulated_norm(x_ref, shift_ref, scale_ref, g_ref, rs if bb == 1 else None)
    h_ref[rs, :] = h
    return h


def _modulated_norm(x_ref, shift_ref, scale_ref, g_ref, rows=None):
    x = x_ref[...] if rows is None else x_ref[:, rows, :]
    ms = jnp.mean(x * x, axis=-1, keepdims=True)
    y = x * lax.rsqrt(ms + RMS_EPS) * g_ref[...]
    h = y * (1.0 + scale_ref[...]) + shift_ref[...]
    bb, tm, d = h.shape
    return h.reshape(bb * tm, d).astype(BF16)


def _bf16_weights(w_ref, emit_ref):
    if emit_ref is None:
        return w_ref[...]
    w = w_ref[...].astype(BF16)
    emit_ref[...] = w
    return w


def _in_even_kernel(x_ref, shift_ref, scale_ref, g_ref, w_ref, cos_ref, sin_ref,
                    o_ref, *rest):
    emit_ref, h_ref = rest if len(rest) == 2 else (None, rest[0])
    n = pl.program_id(2)
    tn = w_ref.shape[1]
    per = D_ATTN // tn
    assert D_POOL == D_ATTN and D_ATTN % tn == 0

    is_gate = ((n >= 3 * per) & (n < 4 * per)) | (n >= 5 * per)
    bb, tm, _ = x_ref.shape

    def step(first):
        for rs in _row_chunks(bb, tm):
            h = _normalized_rows(first, rs, bb, x_ref, shift_ref, scale_ref, g_ref, h_ref)
            acc = _dot(h, _bf16_weights(w_ref, emit_ref))
            cos = cos_ref[rs, :]
            sin = sin_ref[rs, :]
            for j in range(tn // HEAD_DIM):
                cs = slice(j * HEAD_DIM, (j + 1) * HEAD_DIM)
                a = acc[:, cs]
                lin = a * cos + pltpu.roll(a, HEAD_DIM // 2, axis=1) * sin
                o_ref[j, rs, :] = jnp.where(is_gate, _silu(lin), lin)

    pl.when(n == 0)(functools.partial(step, True))
    pl.when(n > 0)(functools.partial(step, False))


def _mod_spec(where, bb, width, col_blk):
    layer, _, mod_blk0 = where
    return pl.BlockSpec((None, bb, 1, width),
                        lambda b, i, n: (layer, mod_blk0 + b, 0, col_blk(n)))


def _rotation_coefficients(cos_rows, sin_rows):
    ones, zeros = jnp.ones_like(cos_rows), jnp.zeros_like(sin_rows)
    return jnp.stack([jnp.stack([cos_rows * QK_SCALE, sin_rows * QK_SCALE]),
                      jnp.stack([cos_rows, sin_rows]),
                      jnp.stack([ones, zeros])])


def _weight_specs(w, li, rows, cols, col_blk):
    flat = lambda b, i, n: (0, col_blk(n))
    if w.ndim == 2:
        return pl.BlockSpec((rows, cols), flat), None, None
    assert w.dtype == F32 and w.shape[1] == rows
    return (pl.BlockSpec((None, rows, cols), lambda b, i, n: (li, 0, col_blk(n))),
            pl.BlockSpec((rows, cols), flat), jax.ShapeDtypeStruct((rows, w.shape[2]), BF16))


def _in_even(x, mods, norm_g, w, coef, where, bb, tm):
    layer, li, _ = where
    bk, t, d = x.shape
    n_out = w.shape[-1]
    tn = 512
    per = D_ATTN // tn
    rows = bb * tm
    grid = (bk // bb, t // tm, n_out // tn)
    coef_spec = lambda which: pl.BlockSpec(
        (None, None, rows, HEAD_DIM),
        lambda b, i, n: (jnp.minimum(n // per, 2), which, i, 0))
    w_spec, emit_spec, emit_shape = _weight_specs(w, li, d, tn, lambda n: n)
    emit = emit_spec is not None
    assert not emit or grid[:2] == (1, 1)
    slabs = tn // HEAD_DIM
    out_specs = [pl.BlockSpec((None, slabs, rows, HEAD_DIM), lambda b, i, n: (b, n, i, 0))]
    out_shape = [jax.ShapeDtypeStruct(
        (bk // bb, n_out // HEAD_DIM, (t // tm) * rows, HEAD_DIM), F32)]
    outs = pl.pallas_call(
        _in_even_kernel,
        grid=grid,
        in_specs=[
            pl.BlockSpec((bb, tm, d), lambda b, i, n: (b, i, 0)),
            _mod_spec(where, bb, d, lambda n: 0),
            _mod_spec(where, bb, d, lambda n: 1),
            pl.BlockSpec((None, 1, d), lambda b, i, n: (layer, 0, 0)),
            w_spec,
            coef_spec(0),
            coef_spec(1),
        ],
        out_specs=out_specs + ([emit_spec] if emit else []),
        out_shape=out_shape + ([emit_shape] if emit else []),
        scratch_shapes=[pltpu.VMEM((rows, d), BF16)],
        compiler_params=_params("arbitrary", "arbitrary", "arbitrary"),
        name="in_proj_even",
    )(x, mods, mods, norm_g, w, coef, coef)
    return outs[0], (outs[1] if emit else None)


def _in_odd_kernel(x_ref, shift_ref, scale_ref, g_ref, wb_ref, wc_ref, wh_ref, wg_ref,
                   cw_ref, pre_ref, o_ref, tail_ref, *rest):
    *emit_refs, h_ref, zbuf_ref = rest
    eb_ref, ec_ref, eh_ref, eg_ref = emit_refs if emit_refs else (None,) * 4
    i = pl.program_id(1)
    n = pl.program_id(2)
    bb, tm, _ = x_ref.shape
    tn = wb_ref.shape[1]
    halo = SUBLANES
    tail_cols = pl.ds(pl.multiple_of(n * tn, tn), tn)

    @pl.when((n == 0) & (i == 0))
    def _():
        tail_ref[...] = pre_ref[...]

    def step(first):
        zbuf_ref[:, 0:halo, :] = tail_ref[:, :, tail_cols]
        cw = cw_ref[...]
        for rs in _row_chunks(bb, tm):
            lo, hi = (rs.start, rs.stop) if bb == 1 else (0, tm)
            h = _normalized_rows(first, rs, bb, x_ref, shift_ref, scale_ref, g_ref, h_ref)
            z = (_dot(h, _bf16_weights(wc_ref, ec_ref))
                 * _dot(h, _bf16_weights(wh_ref, eh_ref))).reshape(bb, hi - lo, tn)
            zbuf_ref[:, halo + lo:halo + hi, :] = z
            yc = zbuf_ref[:, halo - 2 + lo:halo - 2 + hi, :] * cw[0:1]
            yc = yc + zbuf_ref[:, halo - 1 + lo:halo - 1 + hi, :] * cw[1:2]
            yc = yc + z * cw[2:3]
            bg = _dot(h, _bf16_weights(wb_ref, eb_ref)).reshape(bb, hi - lo, tn)
            gate = _dot(h, _bf16_weights(wg_ref, eg_ref)).reshape(bb, hi - lo, tn)
            y = bg * yc * _silu(gate)
            o_ref[rs, :] = y.reshape(rs.stop - rs.start, tn).astype(BF16)
        tail_ref[:, :, tail_cols] = zbuf_ref[:, tm:tm + halo, :]

    pl.when(n == 0)(functools.partial(step, True))
    pl.when(n > 0)(functools.partial(step, False))


def _in_odd(x, mods, norm_g, w, conv_w, prefix, where, bb, tm):
    layer, li, _ = where
    bk, t, d = x.shape
    tn = 256
    nt = D_CONV // tn
    rows = bb * tm
    grid = (bk // bb, t // tm, nt)
    emit = not isinstance(w, (tuple, list))
    assert not emit or grid[:2] == (1, 1)
    if emit:
        specs = [_weight_specs(w, li, d, tn, lambda n, k=k: k * nt + n) for k in range(4)]
        emit_specs = [pl.BlockSpec((d, tn), lambda b, i, n: (0, n))] * 4
        emit_shapes = [jax.ShapeDtypeStruct((d, D_CONV), BF16)] * 4
        w_args = (w,) * 4
    else:
        specs = [_weight_specs(wk, li, d, tn, lambda n: n) for wk in w]
        emit_specs, emit_shapes = [], []
        w_args = tuple(w)
    outs = pl.pallas_call(
        _in_odd_kernel,
        grid=grid,
        in_specs=[
            pl.BlockSpec((bb, tm, d), lambda b, i, n: (b, i, 0)),
            _mod_spec(where, bb, d, lambda n: 0),
            _mod_spec(where, bb, d, lambda n: 1),
            pl.BlockSpec((None, 1, d), lambda b, i, n: (layer, 0, 0)),
            *[s[0] for s in specs],
            pl.BlockSpec((None, CONV_WIDTH, tn), lambda b, i, n: (li, 0, n)),
            pl.BlockSpec((bb, SUBLANES, D_CONV), lambda b, i, n: (b, 0, 0)),
        ],
        out_specs=[
            pl.BlockSpec((None, rows, tn), lambda b, i, n: (b, i, n)),
            pl.BlockSpec((bb, SUBLANES, D_CONV), lambda b, i, n: (b, 0, 0)),
        ] + emit_specs,
        out_shape=[
            jax.ShapeDtypeStruct((bk // bb, (t // tm) * rows, D_CONV), BF16),
            jax.ShapeDtypeStruct((bk, SUBLANES, D_CONV), F32),
        ] + emit_shapes,
        scratch_shapes=[
            pltpu.VMEM((rows, d), BF16),
            pltpu.VMEM((bb, SUBLANES + tm, tn), F32),
        ],
        compiler_params=_params("arbitrary", "arbitrary", "arbitrary"),
        name="in_proj_odd_conv",
    )(x, mods, mods, norm_g, *w_args, conv_w, prefix)
    return outs[0], outs[1], (tuple(outs[2:]) if emit else None)


def _out_proj_kernel(a1_ref, a2_ref, w1_ref, w2_ref, x_ref, gate_ref, fg_ref, o_ref,
                     *emit_refs, final):
    e1_ref, e2_ref = emit_refs if emit_refs else (None, None)
    bb, tm, d = x_ref.shape
    for rs in _row_chunks(bb, tm):
        lo, hi = (rs.start, rs.stop) if bb == 1 else (0, tm)
        acc = _dot(a1_ref[rs, :].astype(BF16), _bf16_weights(w1_ref, e1_ref))
        acc = acc + _dot(a2_ref[rs, :].astype(BF16), _bf16_weights(w2_ref, e2_ref))
        x = x_ref[:, lo:hi, :] + gate_ref[...] * acc.reshape(bb, hi - lo, d)
        if final:
            ms = jnp.mean(x * x, axis=-1, keepdims=True)
            x = x * lax.rsqrt(ms + RMS_EPS) * fg_ref[...]
        o_ref[:, lo:hi, :] = x


def _out_proj(a1, a1_blk, a2, a2_blk, w, x, mods, final_g, where, bb, tm):
    layer, li, mod_blk0 = where
    bk, t, d = x.shape
    rows = bb * tm
    grid = (bk // bb, t // tm)
    emit = not isinstance(w, (tuple, list))
    assert not emit or grid == (1, 1)
    if emit:
        kh = w.shape[1] // 2
        w_specs = [pl.BlockSpec((None, kh, d), lambda b, i, k=k: (li, k, 0)) for k in range(2)]
        emit_specs = [pl.BlockSpec((kh, d), lambda b, i: (0, 0))] * 2
        emit_shapes = [jax.ShapeDtypeStruct((kh, d), BF16)] * 2
        w_args = (w, w)
    else:
        kh = w[0].shape[0]
        w_specs = [pl.BlockSpec((kh, d), lambda b, i: (0, 0))] * 2
        emit_specs, emit_shapes = [], []
        w_args = tuple(w)
    outs = pl.pallas_call(
        functools.partial(_out_proj_kernel, final=layer == DEPTH - 1),
        grid=grid,
        in_specs=[
            pl.BlockSpec((None, rows, kh), lambda b, i: (b, i, a1_blk)),
            pl.BlockSpec((None, rows, kh), lambda b, i: (b, i, a2_blk)),
            *w_specs,
            pl.BlockSpec((bb, tm, d), lambda b, i: (b, i, 0)),
            pl.BlockSpec((None, bb, 1, d), lambda b, i: (layer, mod_blk0 + b, 0, 2)),
            pl.BlockSpec((1, d), lambda b, i: (0, 0)),
        ],
        out_specs=[pl.BlockSpec((bb, tm, d), lambda b, i: (b, i, 0))] + emit_specs,
        out_shape=[jax.ShapeDtypeStruct((bk, t, d), F32)] + emit_shapes,
        compiler_params=_params("arbitrary", "arbitrary"),
        name="out_proj_residual",
    )(a1, a2, *w_args, x, mods, final_g)
    return outs[0], (tuple(outs[1:]) if emit else None)


def _attn_prompt_kernel(q_ref, k_ref, v_ref, ga_ref, bias_ref, o_ref, kt_ref, vt_ref,
                        acc_ref, m_ref, l_ref):
    t = q_ref.shape[0]
    bq = KEYS_PER_PATTERN
    dilations = sorted((d for _, d in PATTERNS), reverse=True)
    assert dilations[-1] == 1

    keep, nh, _ = kt_ref.shape
    head = pl.program_id(1)
    for src_ref, dst_ref in ((k_ref, kt_ref), (v_ref, vt_ref)):
        dst = dst_ref.reshape(keep * nh, HEAD_DIM)
        dst[pl.ds(head, keep, stride=nh), :] = src_ref[pl.ds(t - keep, keep), :]

    for pi, d in enumerate(dilations):
        span = bq * d
        nblk = t // span
        first = pi == 0
        last = pi == len(dilations) - 1

        def rows(ref, start, n, d=d):
            if d == 1:
                return ref[pl.ds(pl.multiple_of(start, bq), n), :]
            return ref[pl.ds(start, n, stride=d), :]

        def put(ref, start, val, d=d):
            if d == 1:
                ref[pl.ds(pl.multiple_of(start, bq), bq), :] = val
            else:
                ref[pl.ds(start, bq, stride=d), :] = val

        def body(it, carry, nblk=nblk, span=span, rows=rows, put=put, first=first, last=last):
            bases, kbases, has_prevs = [], [], []
            for u in range(ATTN_UNROLL):
                itu = it * ATTN_UNROLL + u
                blk = itu % nblk
                base = itu // nblk + blk * span
                has_prev = jnp.minimum(blk, 1)
                bases.append(base)
                kbases.append(base - has_prev * span)
                has_prevs.append(has_prev)
            stack = lambda ref, starts, n: jnp.stack([rows(ref, s, n) for s in starts])

            def keys(ref):
                if nblk == 2 and ATTN_UNROLL % 2 == 0:
                    pairs = [rows(ref, s, 2 * bq).astype(BF16) for s in kbases[::2]]
                    return jnp.stack([pairs[u // 2] for u in range(ATTN_UNROLL)])
                return stack(ref, kbases, 2 * bq).astype(BF16)

            q = stack(q_ref, bases, bq).astype(BF16)
            k = keys(k_ref)
            v = keys(v_ref)
            s = jnp.einsum('uqd,ukd->uqk', q, k, preferred_element_type=F32)
            s = s + jnp.stack([bias_ref[hp] for hp in has_prevs])
            m = jnp.max(jnp.maximum(s[:, :, :bq], s[:, :, bq:]), axis=-1, keepdims=True)
            p = jnp.exp2(s - m)
            l = jnp.sum(p[:, :, :bq] + p[:, :, bq:], axis=-1, keepdims=True)
            num = jnp.einsum('uqk,ukd->uqd', p.astype(BF16), v, preferred_element_type=F32)
            m = jnp.broadcast_to(m, num.shape)
            l = jnp.broadcast_to(l, num.shape)
            if not first:
                m_old = stack(m_ref, bases, bq)
                m_new = jnp.maximum(m_old, m)
                a_old = jnp.exp2(m_old - m_new)
                a_new = jnp.exp2(m - m_new)
                num = a_old * stack(acc_ref, bases, bq) + a_new * num
                l = a_old * stack(l_ref, bases, bq) + a_new * l
                m = m_new
            for u, base in enumerate(bases):
                if last:
                    rows_out = pl.ds(pl.multiple_of(base, bq), bq)
                    o_ref[rows_out, :] = (num[u] / l[u] * ga_ref[rows_out, :]).astype(BF16)
                else:
                    put(acc_ref, base, num[u])
                    put(m_ref, base, m[u])
                    put(l_ref, base, l[u])
            return carry

        lax.fori_loop(0, t // (bq * ATTN_UNROLL), body, 0, unroll=2)


def _attn_prompt(proj):
    b, _, t, _ = proj.shape
    for w, d in PATTERNS:
        assert w // d == KEYS_PER_PATTERN and t % (2 * KEYS_PER_PATTERN * d) == 0
    assert (t // KEYS_PER_PATTERN) % ATTN_UNROLL == 0
    col = lambda off: pl.BlockSpec((None, None, t, HEAD_DIM),
                                   lambda bi, h, off=off: (bi, off + h, 0, 0))
    bq = KEYS_PER_PATTERN
    qi = jnp.arange(bq, dtype=jnp.int32)[:, None]
    kj = jnp.arange(2 * bq, dtype=jnp.int32)[None, :]
    visible = jnp.stack([kj <= qi, (kj >= qi) & (kj <= qi + bq)])
    bias = jnp.where(visible, 0.0, NEG_INF).astype(F32)
    keep = min(WINDOW_MAX, t)
    tail = pl.BlockSpec((None, keep, N_HEADS, HEAD_DIM), lambda bi, h: (bi, 0, 0, 0),
                        pipeline_mode=pl.Buffered(1))
    tail_shape = jax.ShapeDtypeStruct((b, keep, N_HEADS, HEAD_DIM), F32)
    return pl.pallas_call(
        _attn_prompt_kernel,
        grid=(b, N_HEADS),
        in_specs=[col(0), col(N_HEADS), col(2 * N_HEADS), col(3 * N_HEADS),
                  pl.BlockSpec((2, bq, 2 * bq), lambda bi, h: (0, 0, 0))],
        out_specs=(pl.BlockSpec((None, t, HEAD_DIM), lambda bi, h: (bi, 0, h)), tail, tail),
        out_shape=(jax.ShapeDtypeStruct((b, t, D_ATTN), BF16), tail_shape, tail_shape),
        scratch_shapes=[pltpu.VMEM((t, HEAD_DIM), F32)] * 3,
        compiler_params=_params("arbitrary", "arbitrary"),
        name="attn_prompt",
    )(proj, proj, proj, proj, bias)


def _pattern_count(delta):
    cnt = jnp.zeros(delta.shape, F32)
    for w, d in PATTERNS:
        assert d & (d - 1) == 0
        ok = (delta >= 0) & (delta <= w) & (jnp.bitwise_and(delta, d - 1) == 0)
        cnt = cnt + ok.astype(F32)
    return cnt


def _sample_key_weights(nh, tq, wb, n_groups, n_dense):
    nq = nh * tq
    r = jnp.arange(nq, dtype=jnp.int32)[:, None]

    def window(groups, group0, residue0):
        c = jnp.arange(groups * SUBLANES * nh, dtype=jnp.int32)[None, :]
        key_head = c % nh
        residue = residue0 + (c // nh) % SUBLANES
        key_row = (group0 + c // (nh * SUBLANES)) * SAMPLE_GROUP + residue
        delta = wb + r % tq - key_row
        return jnp.where(r // tq == key_head, _pattern_count(delta), 0.0)

    c = jnp.arange(nq, dtype=jnp.int32)[None, :]
    new = jnp.where(r // tq == c // tq, _pattern_count(r % tq - c % tq), 0.0)
    return window(n_groups, 0, 0), window(n_dense, n_groups - n_dense, SUBLANES), new


def _attn_sample_kernel(q_ref, kn_ref, vn_ref, ga_ref, ka_ref, kb_ref, va_ref, vb_ref,
                        wa_ref, wb_ref, wn_ref, o_ref):
    nh, tq, _ = q_ref.shape
    nq = nh * tq

    def heads_to_rows(ref):
        return ref[...].reshape(nq, HEAD_DIM)

    def flat(ref):
        g, j, h, d = ref.shape
        return ref[...].reshape(g * j * h, d).astype(BF16)

    parts = (
        (flat(ka_ref), flat(va_ref), wa_ref[...]),
        (flat(kb_ref), flat(vb_ref), wb_ref[...]),
        (heads_to_rows(kn_ref).astype(BF16), heads_to_rows(vn_ref).astype(BF16), wn_ref[...]),
    )
    q = heads_to_rows(q_ref).astype(BF16)
    scores = [jnp.where(wgt > 0, _dot_nt(q, k), NEG_INF) for k, _, wgt in parts]
    m = functools.reduce(jnp.maximum, [jnp.max(s, axis=-1, keepdims=True) for s in scores])
    den = jnp.zeros((nq, 1), F32)
    num = jnp.zeros((nq, HEAD_DIM), F32)
    for s, (_, v, wgt) in zip(scores, parts):
        p = wgt * jnp.exp2(s - m)
        den = den + jnp.sum(p, axis=-1, keepdims=True)
        num = num + _dot(p.astype(BF16), v)
    out = num / den
    for h in range(nh):
        cs = slice(h * HEAD_DIM, (h + 1) * HEAD_DIM)
        o_ref[:, cs] = out[h * tq:(h + 1) * tq] * ga_ref[h]


def _attn_sample(proj, win_k, win_v, li, n_seq, tq):
    wb = win_k.shape[2]
    dense_reach = max(w for w, d in PATTERNS if d < SAMPLE_GROUP)
    assert max(d for _, d in PATTERNS) == SAMPLE_GROUP == 2 * SUBLANES and tq == SUBLANES
    assert wb % SAMPLE_GROUP == 0 and dense_reach % SAMPLE_GROUP == 0 and wb % dense_reach == 0
    n_groups = wb // SAMPLE_GROUP
    n_dense = dense_reach // SAMPLE_GROUP
    win_k = win_k.reshape(-1, n_seq, n_groups, SAMPLE_GROUP, N_HEADS, HEAD_DIM)
    win_v = win_v.reshape(-1, n_seq, n_groups, SAMPLE_GROUP, N_HEADS, HEAD_DIM)
    col = lambda blk: pl.BlockSpec((None, N_HEADS, tq, HEAD_DIM), lambda s, blk=blk: (0, blk, s, 0))
    sparse = pl.BlockSpec((None, None, n_groups, SUBLANES, N_HEADS, HEAD_DIM),
                          lambda s: (li, s, 0, 0, 0, 0))
    dense = pl.BlockSpec((None, None, n_dense, SUBLANES, N_HEADS, HEAD_DIM),
                         lambda s: (li, s, n_groups // n_dense - 1, 1, 0, 0))
    key_weights = _sample_key_weights(N_HEADS, tq, wb, n_groups, n_dense)
    whole = lambda a: pl.BlockSpec(a.shape, lambda s: (0, 0))
    return pl.pallas_call(
        _attn_sample_kernel,
        grid=(n_seq,),
        in_specs=[col(0), col(1), col(2), col(3), sparse, dense, sparse, dense,
                  *[whole(a) for a in key_weights]],
        out_specs=pl.BlockSpec((None, tq, D_ATTN), lambda s: (0, s, 0)),
        out_shape=jax.ShapeDtypeStruct((1, n_seq * tq, D_ATTN), F32),
        compiler_params=_params("arbitrary"),
        name="attn_sample",
    )(proj, proj, proj, proj, win_k, win_k, win_v, win_v, *key_weights)


def _pool_kernel(u_ref, halo_ref, pre_ref, gp_ref, w_ref, s_ref, o_ref, ext_ref, tmp_ref,
                 *, pos0):
    i = pl.program_id(1)
    _, tm, _ = u_ref.shape
    per_group = POOL_GROUP // HEAD_DIM
    pad = SUBLANES
    first = pad + POOL_MAX
    n_ext = first + tm
    assert POOL_MAX // 2 <= pad and n_ext == ext_ref.shape[1]
    lanes = lambda slabs: jnp.concatenate([slabs[s] for s in range(per_group)], axis=-1)

    zeros = jnp.zeros((ext_ref.shape[0], pad, HEAD_DIM), F32)
    ext_ref[:, 0:pad, :] = zeros
    tmp_ref[:, 0:pad, :] = zeros

    @pl.when(i == 0)
    def _():
        for s in range(ext_ref.shape[0]):
            ext_ref[s, pad:first, :] = pre_ref[:, s * HEAD_DIM:(s + 1) * HEAD_DIM]

    @pl.when(i > 0)
    def _():
        ext_ref[:, pad:first, :] = halo_ref[...]

    ext_ref[:, first:, :] = u_ref[...]
    pos = pos0 + i * tm + lax.broadcasted_iota(jnp.int32, (1, tm, 1), 1)

    for gi, w in enumerate(POOL_WINDOWS):
        assert w & (w - 1) == 0
        sl = slice(gi * per_group, (gi + 1) * per_group)
        cols = slice(gi * POOL_GROUP, (gi + 1) * POOL_GROUP)
        u = u_ref[sl]
        src, dst = ext_ref, tmp_ref
        span = 1
        while span < w:
            dst[sl, pad:n_ext, :] = src[sl, pad:n_ext, :] + src[sl, pad - span:n_ext - span, :]
            src, dst = dst, src
            span *= 2
        win = src[sl, first:n_ext, :]
        inv_cnt = 1.0 / jnp.minimum(w, pos + 1).astype(F32)
        pooled = lanes(win * inv_cnt - u)
        mixed = _dot(pooled.astype(BF16), w_ref[gi].astype(BF16)) * s_ref[:, cols]
        o_ref[:, cols] = (mixed * lanes(gp_ref[sl])).astype(o_ref.dtype)


def _pool(proj, prefix, pool_w, pool_scale, li, seq_rows, tm, pos0, out_dtype):
    g_, _, r_, _ = proj.shape
    n_seq = g_ * (r_ // seq_rows)
    nt = seq_rows // tm
    per_g = r_ // seq_rows
    ng = len(POOL_WINDOWS)
    n_slab = D_POOL // HEAD_DIM
    u_blk, gp_blk = 4 * D_ATTN // D_POOL, 4 * D_ATTN // D_POOL + 1
    hb = tm // POOL_MAX

    def slab_map(blk):
        return lambda s, i: (s // per_g, blk, (s % per_g) * nt + i, 0)

    def halo_map(s, i):
        return (s // per_g, u_blk, jnp.maximum(((s % per_g) * nt + i) * hb - 1, 0), 0)

    return pl.pallas_call(
        functools.partial(_pool_kernel, pos0=pos0),
        grid=(n_seq, nt),
        in_specs=[
            pl.BlockSpec((None, n_slab, tm, HEAD_DIM), slab_map(u_blk)),
            pl.BlockSpec((None, n_slab, POOL_MAX, HEAD_DIM), halo_map),
            pl.BlockSpec((None, POOL_MAX, D_POOL), lambda s, i: (s, 0, 0)),
            pl.BlockSpec((None, n_slab, tm, HEAD_DIM), slab_map(gp_blk)),
            pl.BlockSpec((None, ng, POOL_GROUP, POOL_GROUP), lambda s, i: (li, 0, 0, 0)),
            pl.BlockSpec((None, 1, D_POOL), lambda s, i: (li, 0, 0)),
        ],
        out_specs=pl.BlockSpec(
            (None, tm, D_POOL), lambda s, i: (s // per_g, (s % per_g) * nt + i, 0)),
        out_shape=jax.ShapeDtypeStruct((g_, r_, D_POOL), out_dtype),
        scratch_shapes=[pltpu.VMEM((n_slab, SUBLANES + POOL_MAX + tm, HEAD_DIM), F32)] * 2,
        compiler_params=_params("arbitrary", "arbitrary"),
        name="pool_mixer",
    )(proj, proj, prefix, proj, pool_w, pool_scale)


def _trunk(x, mods, mod_row0, pos0, state, weights, matmul_weights, bb, tm, tm_out):
    norm_g, pool_w, pool_scale, conv_w, final_g = weights
    bk, t, _ = x.shape
    fresh = state is None
    assert mod_row0 % bb == 0
    cos, sin = _rope_tables(t, pos0)
    if bb > 1:
        cos, sin = jnp.tile(cos, (bb, 1)), jnp.tile(sin, (bb, 1))
    coef = _rotation_coefficients(cos, sin)
    new_k, new_v, new_pool, new_conv = [], [], [], []
    made = {}
    for layer in range(DEPTH):
        li = layer // 2
        where = (layer, li, mod_row0 // bb)
        if layer % 2 == 0:
            proj, made["in_even", li] = _in_even(
                x, mods, norm_g, matmul_weights["in_even", li], coef, where, bb, tm)

            def seq_rows(first_slab, n_last, proj=proj):
                g = proj.shape[0]
                slabs = slice(first_slab, first_slab + N_HEADS)
                if g == bk:
                    return jnp.transpose(proj[:, slabs, t - n_last:], (0, 2, 1, 3))
                s = proj[:, slabs].reshape(g, N_HEADS, bk // g, t, HEAD_DIM)[:, :, :, t - n_last:]
                return jnp.transpose(s, (0, 2, 3, 1, 4)).reshape(bk, n_last, N_HEADS, HEAD_DIM)

            if fresh:
                pool_prefix = jnp.zeros((bk, POOL_MAX - 1, D_POOL), F32)
                attn, k, v = _attn_prompt(proj)
                u_tail = seq_rows(4 * N_HEADS, POOL_MAX - 1).reshape(bk, -1, D_POOL)
            else:
                win_k, win_v, pool_st, _ = state
                pool_prefix = pool_st[li]
                attn = _attn_sample(proj, win_k, win_v, li, bk, t)
                k = seq_rows(N_HEADS, t)
                v = seq_rows(2 * N_HEADS, t)
                u_tail = jnp.concatenate(
                    [pool_prefix, seq_rows(4 * N_HEADS, t).reshape(bk, t, D_POOL)],
                    axis=1)[:, -(POOL_MAX - 1):]
            new_pool.append(u_tail)
            new_k.append(k)
            new_v.append(v)
            prefix = jnp.pad(pool_prefix, ((0, 0), (1, 0), (0, 0)))
            pooled = _pool(proj, prefix, pool_w, pool_scale, li, t,
                           min(t, 1024), pos0, BF16 if fresh else F32)
            x, made["out_even", li] = _out_proj(
                attn, 0, pooled, 0, matmul_weights["out_even", li], x, mods, final_g, where,
                bb, tm_out)
        else:
            if fresh:
                prefix = jnp.zeros((bk, SUBLANES, D_CONV), F32)
            else:
                prefix = jnp.pad(state[3][li], ((0, 0), (SUBLANES - (CONV_WIDTH - 1), 0), (0, 0)))
            y, tail, made["in_odd", li] = _in_odd(
                x, mods, norm_g, matmul_weights["in_odd", li], conv_w, prefix, where, bb, tm)
            new_conv.append(tail[:, SUBLANES - (CONV_WIDTH - 1):])
            x, made["out_odd", li] = _out_proj(
                y, 0, y, 1, matmul_weights["out_odd", li], x, mods, final_g, where, bb, tm_out)
    outs = (x, jnp.stack(new_k), jnp.stack(new_v), jnp.stack(new_pool), jnp.stack(new_conv))
    return outs, made


def kernel(x_prompt, x_sample, c_prompt, c_sample, state_win_k, state_win_v, state_pool,
           state_conv, ada_w, ada_b, norm_g, w_in_even, w_out_even, pool_w, pool_scale,
           w_in_odd, conv_w, w_out_odd, final_g):
    nb, t_p, _ = x_prompt.shape
    ns, t_s, _ = x_sample.shape
    n_c = nb + ns
    c_rows = -(-n_c // SUBLANES) * SUBLANES
    c_all = jnp.pad(jnp.concatenate([c_sample, c_prompt], axis=0), ((0, c_rows - n_c), (0, 0)))
    mods = _ada_modulation(c_all, ada_w, ada_b)[:, :, None, :]
    weights = (norm_g[:, None, :], pool_w, pool_scale[:, None, :], conv_w, final_g[None, :])
    stacks = {"in_even": w_in_even, "out_even": w_out_even, "in_odd": w_in_odd,
              "out_odd": w_out_odd}
    f32_weights = {(kind, li): w for kind, w in stacks.items() for li in range(w.shape[0])}

    state = (state_win_k, state_win_v, state_pool, state_conv)
    (y_s, ksm, vsm, pools, convs), bf16_weights = _trunk(
        x_sample, mods, 0, PAST_LEN, state, weights, f32_weights, ns, t_s, t_s)
    (y_p, kp, vp, poolp, convp), _ = _trunk(
        x_prompt, mods, ns, 0, None, weights, bf16_weights, 1, 1024, 512)
    return (y_p, y_s, kp, vp, poolp, convp, ksm, vsm, pools, convs)
```

```python
import functools
import math

import jax
import jax.numpy as jnp
from jax import lax
from jax.experimental import pallas as pl
from jax.experimental.pallas import tpu as pltpu

F32 = jnp.float32
BF16 = jnp.bfloat16

D_MODEL = 2048
DEPTH = 4
PAST_LEN = 16384
D_ATTN = D_MODEL // 2
HEAD_DIM = 128
N_HEADS = D_ATTN // HEAD_DIM
PATTERNS = ((128, 1), (512, 4), (2048, 16))
KEYS_PER_PATTERN = 128
ATTN_UNROLL = 8
SAMPLE_GROUP = max(d for _, d in PATTERNS)
MATMUL_ROW_CHUNK = 128
WINDOW_MAX = max(w for w, _ in PATTERNS)
ROPE_THETA = 10000.0
D_POOL = D_MODEL // 2
POOL_WINDOWS = (2, 4, 8, 16)
POOL_GROUP = D_POOL // len(POOL_WINDOWS)
POOL_MAX = max(POOL_WINDOWS)
D_CONV = D_MODEL
CONV_WIDTH = 3
D_IN_EVEN = 4 * D_ATTN + 2 * D_POOL
RMS_EPS = 1e-6
NEG_INF = -1e30
QK_SCALE = HEAD_DIM ** -0.5 * math.log2(math.e)

SUBLANES = 8
VMEM_LIMIT = 48 * 1024 * 1024


def _params(*semantics):
    return pltpu.CompilerParams(dimension_semantics=semantics,
                                vmem_limit_bytes=VMEM_LIMIT)


def _silu(v):
    return v / (1.0 + jnp.exp(-v))


def _dot(a, b):
    return jnp.dot(a, b, preferred_element_type=F32)


def _dot_nt(a, b):
    return lax.dot_general(a, b, (((1,), (1,)), ((), ())), preferred_element_type=F32)


def _rope_kernel(inv_ref, cos_ref, sin_ref, *, pos0):
    t = cos_ref.shape[0]
    pos = (lax.broadcasted_iota(jnp.int32, (t, HEAD_DIM), 0) + pos0).astype(F32)
    lane = lax.broadcasted_iota(jnp.int32, (t, HEAD_DIM), 1)
    ang = pos * inv_ref[...]
    s = jnp.sin(ang)
    cos_ref[...] = jnp.cos(ang)
    sin_ref[...] = jnp.where(lane < HEAD_DIM // 2, -s, s)


def _rope_tables(t, pos0):
    half = HEAD_DIM // 2
    inv = jnp.power(ROPE_THETA, -jnp.arange(half, dtype=F32) / half)
    inv = jnp.concatenate([inv, inv])[None, :]
    return pl.pallas_call(
        functools.partial(_rope_kernel, pos0=pos0),
        out_shape=(jax.ShapeDtypeStruct((t, HEAD_DIM), F32),
                   jax.ShapeDtypeStruct((t, HEAD_DIM), F32)),
        name="rope_tables",
    )(inv)


def _ada_kernel(c_ref, w_ref, b_ref, o_ref):
    a = _silu(c_ref[...]).astype(BF16)
    o_ref[...] = _dot(a, w_ref[...].astype(BF16)) + b_ref[...]


def _ada_modulation(c_all, ada_w, ada_b):
    rows = c_all.shape[0]
    n_out = ada_w.shape[2]
    tn = 1024
    return pl.pallas_call(
        _ada_kernel,
        grid=(DEPTH, n_out // tn),
        in_specs=[
            pl.BlockSpec((rows, D_MODEL), lambda l, n: (0, 0)),
            pl.BlockSpec((None, D_MODEL, tn), lambda l, n: (l, 0, n)),
            pl.BlockSpec((None, 1, tn), lambda l, n: (l, 0, n)),
        ],
        out_specs=pl.BlockSpec((None, rows, tn), lambda l, n: (l, 0, n)),
        out_shape=jax.ShapeDtypeStruct((DEPTH, rows, n_out), F32),
        compiler_params=_params("arbitrary", "arbitrary"),
        name="ada_modulation",
    )(c_all, ada_w, ada_b.reshape(DEPTH, 1, n_out))


def _row_chunks(bb, tm):
    chunk = min(tm, MATMUL_ROW_CHUNK) if bb == 1 else bb * tm
    return [slice(c * chunk, (c + 1) * chunk) for c in range(bb * tm // chunk)]


def _normalized_rows(first, rs, bb, x_ref, shift_ref, scale_ref, g_ref, h_ref):
    if not first:
        return h_ref[rs, :]
    h = _modulated_norm(x_ref, shift_ref, scale_ref, g_ref, rs if bb == 1 else None)
    h_ref[rs, :] = h
    return h


def _modulated_norm(x_ref, shift_ref, scale_ref, g_ref, rows=None):
    x = x_ref[...] if rows is None else x_ref[:, rows, :]
    ms = jnp.mean(x * x, axis=-1, keepdims=True)
    y = x * lax.rsqrt(ms + RMS_EPS) * g_ref[...]
    h = y * (1.0 + scale_ref[...]) + shift_ref[...]
    bb, tm, d = h.shape
    return h.reshape(bb * tm, d).astype(BF16)


def _bf16_weights(w_ref, emit_ref):
    if emit_ref is None:
        return w_ref[...]
    w = w_ref[...].astype(BF16)
    emit_ref[...] = w
    return w


def _in_even_kernel(x_ref, shift_ref, scale_ref, g_ref, w_ref, cos_ref, sin_ref,
                    o_ref, *rest):
    emit_ref, h_ref = rest if len(rest) == 2 else (None, rest[0])
    n = pl.program_id(2)
    tn = w_ref.shape[1]
    per = D_ATTN // tn
    assert D_POOL == D_ATTN and D_ATTN % tn == 0

    is_gate = ((n >= 3 * per) & (n < 4 * per)) | (n >= 5 * per)
    bb, tm, _ = x_ref.shape

    def step(first):
        for rs in _row_chunks(bb, tm):
            h = _normalized_rows(first, rs, bb, x_ref, shift_ref, scale_ref, g_ref, h_ref)
            acc = _dot(h, _bf16_weights(w_ref, emit_ref))
            cos = cos_ref[rs, :]
            sin = sin_ref[rs, :]
            for j in range(tn // HEAD_DIM):
                cs = slice(j * HEAD_DIM, (j + 1) * HEAD_DIM)
                a = acc[:, cs]
                lin = a * cos + pltpu.roll(a, HEAD_DIM // 2, axis=1) * sin
                o_ref[j, rs, :] = jnp.where(is_gate, _silu(lin), lin)

    pl.when(n == 0)(functools.partial(step, True))
    pl.when(n > 0)(functools.partial(step, False))


def _mod_spec(where, bb, width, col_blk):
    layer, _, mod_blk0 = where
    return pl.BlockSpec((None, bb, 1, width),
                        lambda b, i, n: (layer, mod_blk0 + b, 0, col_blk(n)))


def _rotation_coefficients(cos_rows, sin_rows):
    ones, zeros = jnp.ones_like(cos_rows), jnp.zeros_like(sin_rows)
    return jnp.stack([jnp.stack([cos_rows * QK_SCALE, sin_rows * QK_SCALE]),
                      jnp.stack([cos_rows, sin_rows]),
                      jnp.stack([ones, zeros])])


def _weight_specs(w, li, rows, cols, col_blk):
    flat = lambda b, i, n: (0, col_blk(n))
    if w.ndim == 2:
        return pl.BlockSpec((rows, cols), flat), None, None
    assert w.dtype == F32 and w.shape[1] == rows
    return (pl.BlockSpec((None, rows, cols), lambda b, i, n: (li, 0, col_blk(n))),
            pl.BlockSpec((rows, cols), flat), jax.ShapeDtypeStruct((rows, w.shape[2]), BF16))


def _in_even(x, mods, norm_g, w, coef, where, bb, tm):
    layer, li, _ = where
    bk, t, d = x.shape
    n_out = w.shape[-1]
    tn = 512
    per = D_ATTN // tn
    rows = bb * tm
    grid = (bk // bb, t // tm, n_out // tn)
    coef_spec = lambda which: pl.BlockSpec(
        (None, None, rows, HEAD_DIM),
        lambda b, i, n: (jnp.minimum(n // per, 2), which, i, 0))
    w_spec, emit_spec, emit_shape = _weight_specs(w, li, d, tn, lambda n: n)
    emit = emit_spec is not None
    assert not emit or grid[:2] == (1, 1)
    slabs = tn // HEAD_DIM
    out_specs = [pl.BlockSpec((None, slabs, rows, HEAD_DIM), lambda b, i, n: (b, n, i, 0))]
    out_shape = [jax.ShapeDtypeStruct(
        (bk // bb, n_out // HEAD_DIM, (t // tm) * rows, HEAD_DIM), F32)]
    outs = pl.pallas_call(
        _in_even_kernel,
        grid=grid,
        in_specs=[
            pl.BlockSpec((bb, tm, d), lambda b, i, n: (b, i, 0)),
            _mod_spec(where, bb, d, lambda n: 0),
            _mod_spec(where, bb, d, lambda n: 1),
            pl.BlockSpec((None, 1, d), lambda b, i, n: (layer, 0, 0)),
            w_spec,
            coef_spec(0),
            coef_spec(1),
        ],
        out_specs=out_specs + ([emit_spec] if emit else []),
        out_shape=out_shape + ([emit_shape] if emit else []),
        scratch_shapes=[pltpu.VMEM((rows, d), BF16)],
        compiler_params=_params("arbitrary", "arbitrary", "arbitrary"),
        name="in_proj_even",
    )(x, mods, mods, norm_g, w, coef, coef)
    return outs[0], (outs[1] if emit else None)


def _in_odd_kernel(x_ref, shift_ref, scale_ref, g_ref, wb_ref, wc_ref, wh_ref, wg_ref,
                   cw_ref, pre_ref, o_ref, tail_ref, *rest):
    *emit_refs, h_ref, zbuf_ref = rest
    eb_ref, ec_ref, eh_ref, eg_ref = emit_refs if emit_refs else (None,) * 4
    i = pl.program_id(1)
    n = pl.program_id(2)
    bb, tm, _ = x_ref.shape
    tn = wb_ref.shape[1]
    halo = SUBLANES
    tail_cols = pl.ds(pl.multiple_of(n * tn, tn), tn)

    @pl.when((n == 0) & (i == 0))
    def _():
        tail_ref[...] = pre_ref[...]

    def step(first):
        zbuf_ref[:, 0:halo, :] = tail_ref[:, :, tail_cols]
        cw = cw_ref[...]
        for rs in _row_chunks(bb, tm):
            lo, hi = (rs.start, rs.stop) if bb == 1 else (0, tm)
            h = _normalized_rows(first, rs, bb, x_ref, shift_ref, scale_ref, g_ref, h_ref)
            z = (_dot(h, _bf16_weights(wc_ref, ec_ref))
                 * _dot(h, _bf16_weights(wh_ref, eh_ref))).reshape(bb, hi - lo, tn)
            zbuf_ref[:, halo + lo:halo + hi, :] = z
            yc = zbuf_ref[:, halo - 2 + lo:halo - 2 + hi, :] * cw[0:1]
            yc = yc + zbuf_ref[:, halo - 1 + lo:halo - 1 + hi, :] * cw[1:2]
            yc = yc + z * cw[2:3]
            bg = _dot(h, _bf16_weights(wb_ref, eb_ref)).reshape(bb, hi - lo, tn)
            gate = _dot(h, _bf16_weights(wg_ref, eg_ref)).reshape(bb, hi - lo, tn)
            y = bg * yc * _silu(gate)
            o_ref[rs, :] = y.reshape(rs.stop - rs.start, tn).astype(BF16)
        tail_ref[:, :, tail_cols] = zbuf_ref[:, tm:tm + halo, :]

    pl.when(n == 0)(functools.partial(step, True))
    pl.when(n > 0)(functools.partial(step, False))


def _in_odd(x, mods, norm_g, w, conv_w, prefix, where, bb, tm):
    layer, li, _ = where
    bk, t, d = x.shape
    tn = 256
    nt = D_CONV // tn
    rows = bb * tm
    grid = (bk // bb, t // tm, nt)
    emit = not isinstance(w, (tuple, list))
    assert not emit or grid[:2] == (1, 1)
    if emit:
        specs = [_weight_specs(w, li, d, tn, lambda n, k=k: k * nt + n) for k in range(4)]
        emit_specs = [pl.BlockSpec((d, tn), lambda b, i, n: (0, n))] * 4
        emit_shapes = [jax.ShapeDtypeStruct((d, D_CONV), BF16)] * 4
        w_args = (w,) * 4
    else:
        specs = [_weight_specs(wk, li, d, tn, lambda n: n) for wk in w]
        emit_specs, emit_shapes = [], []
        w_args = tuple(w)
    outs = pl.pallas_call(
        _in_odd_kernel,
        grid=grid,
        in_specs=[
            pl.BlockSpec((bb, tm, d), lambda b, i, n: (b, i, 0)),
            _mod_spec(where, bb, d, lambda n: 0),
            _mod_spec(where, bb, d, lambda n: 1),
            pl.BlockSpec((None, 1, d), lambda b, i, n: (layer, 0, 0)),
            *[s[0] for s in specs],
            pl.BlockSpec((None, CONV_WIDTH, tn), lambda b, i, n: (li, 0, n)),
            pl.BlockSpec((bb, SUBLANES, D_CONV), lambda b, i, n: (b, 0, 0)),
        ],
        out_specs=[
            pl.BlockSpec((None, rows, tn), lambda b, i, n: (b, i, n)),
            pl.BlockSpec((bb, SUBLANES, D_CONV), lambda b, i, n: (b, 0, 0)),
        ] + emit_specs,
        out_shape=[
            jax.ShapeDtypeStruct((bk // bb, (t // tm) * rows, D_CONV), BF16),
            jax.ShapeDtypeStruct((bk, SUBLANES, D_CONV), F32),
        ] + emit_shapes,
        scratch_shapes=[
            pltpu.VMEM((rows, d), BF16),
            pltpu.VMEM((bb, SUBLANES + tm, tn), F32),
        ],
        compiler_params=_params("arbitrary", "arbitrary", "arbitrary"),
        name="in_proj_odd_conv",
    )(x, mods, mods, norm_g, *w_args, conv_w, prefix)
    return outs[0], outs[1], (tuple(outs[2:]) if emit else None)


def _out_proj_kernel(a1_ref, a2_ref, w1_ref, w2_ref, x_ref, gate_ref, fg_ref, o_ref,
                     *emit_refs, final):
    e1_ref, e2_ref = emit_refs if emit_refs else (None, None)
    bb, tm, d = x_ref.shape
    for rs in _row_chunks(bb, tm):
        lo, hi = (rs.start, rs.stop) if bb == 1 else (0, tm)
        acc = _dot(a1_ref[rs, :].astype(BF16), _bf16_weights(w1_ref, e1_ref))
        acc = acc + _dot(a2_ref[rs, :].astype(BF16), _bf16_weights(w2_ref, e2_ref))
        x = x_ref[:, lo:hi, :] + gate_ref[...] * acc.reshape(bb, hi - lo, d)
        if final:
            ms = jnp.mean(x * x, axis=-1, keepdims=True)
            x = x * lax.rsqrt(ms + RMS_EPS) * fg_ref[...]
        o_ref[:, lo:hi, :] = x


def _out_proj(a1, a1_blk, a2, a2_blk, w, x, mods, final_g, where, bb, tm):
    layer, li, mod_blk0 = where
    bk, t, d = x.shape
    rows = bb * tm
    grid = (bk // bb, t // tm)
    emit = not isinstance(w, (tuple, list))
    assert not emit or grid == (1, 1)
    if emit:
        kh = w.shape[1] // 2
        w_specs = [pl.BlockSpec((None, kh, d), lambda b, i, k=k: (li, k, 0)) for k in range(2)]
        emit_specs = [pl.BlockSpec((kh, d), lambda b, i: (0, 0))] * 2
        emit_shapes = [jax.ShapeDtypeStruct((kh, d), BF16)] * 2
        w_args = (w, w)
    else:
        kh = w[0].shape[0]
        w_specs = [pl.BlockSpec((kh, d), lambda b, i: (0, 0))] * 2
        emit_specs, emit_shapes = [], []
        w_args = tuple(w)
    outs = pl.pallas_call(
        functools.partial(_out_proj_kernel, final=layer == DEPTH - 1),
        grid=grid,
        in_specs=[
            pl.BlockSpec((None, rows, kh), lambda b, i: (b, i, a1_blk)),
            pl.BlockSpec((None, rows, kh), lambda b, i: (b, i, a2_blk)),
            *w_specs,
            pl.BlockSpec((bb, tm, d), lambda b, i: (b, i, 0)),
            pl.BlockSpec((None, bb, 1, d), lambda b, i: (layer, mod_blk0 + b, 0, 2)),
            pl.BlockSpec((1, d), lambda b, i: (0, 0)),
        ],
        out_specs=[pl.BlockSpec((bb, tm, d), lambda b, i: (b, i, 0))] + emit_specs,
        out_shape=[jax.ShapeDtypeStruct((bk, t, d), F32)] + emit_shapes,
        compiler_params=_params("arbitrary", "arbitrary"),
        name="out_proj_residual",
    )(a1, a2, *w_args, x, mods, final_g)
    return outs[0], (tuple(outs[1:]) if emit else None)


def _attn_prompt_kernel(q_ref, k_ref, v_ref, ga_ref, bias_ref, o_ref, kt_ref, vt_ref,
                        *scratch):
    t = q_ref.shape[0]
    bq = KEYS_PER_PATTERN
    dilations = sorted((d for _, d in PATTERNS), reverse=True)
    assert dilations[-1] == 1 and len(scratch) == 2 * (len(dilations) - 1)
    out_refs, lse_refs = scratch[0::2], scratch[1::2]

    keep, nh, _ = kt_ref.shape
    head = pl.program_id(1)
    for src_ref, dst_ref in ((k_ref, kt_ref), (v_ref, vt_ref)):
        dst = dst_ref.reshape(keep * nh, HEAD_DIM)
        dst[pl.ds(head, keep, stride=nh), :] = src_ref[pl.ds(t - keep, keep), :]

    for pi, d in enumerate(dilations):
        span = bq * d
        nblk = t // span
        last = pi == len(dilations) - 1

        def rows(ref, start, n, d=d):
            if d == 1:
                return ref[pl.ds(pl.multiple_of(start, bq), n), :]
            return ref[pl.ds(start, n, stride=d), :]

        def put(ref, start, val, d=d):
            if d == 1:
                ref[pl.ds(pl.multiple_of(start, bq), bq), :] = val
            else:
                ref[pl.ds(start, bq, stride=d), :] = val

        def body(it, carry, pi=pi, nblk=nblk, span=span, rows=rows, put=put, last=last):
            bases, kbases, has_prevs = [], [], []
            for u in range(ATTN_UNROLL):
                itu = it * ATTN_UNROLL + u
                blk = itu % nblk
                base = itu // nblk + blk * span
                has_prev = jnp.minimum(blk, 1)
                bases.append(base)
                kbases.append(base - has_prev * span)
                has_prevs.append(has_prev)
            stack = lambda ref, starts, n: jnp.stack([rows(ref, s, n) for s in starts])

            def keys(ref):
                if nblk == 2 and ATTN_UNROLL % 2 == 0:
                    pairs = [rows(ref, s, 2 * bq).astype(BF16) for s in kbases[::2]]
                    return jnp.stack([pairs[u // 2] for u in range(ATTN_UNROLL)])
                return stack(ref, kbases, 2 * bq).astype(BF16)

            q = stack(q_ref, bases, bq).astype(BF16)
            k = keys(k_ref)
            v = keys(v_ref)
            s = jnp.einsum('uqd,ukd->uqk', q, k, preferred_element_type=F32)
            s = s + jnp.stack([bias_ref[hp] for hp in has_prevs])
            m = jnp.max(jnp.maximum(s[:, :, :bq], s[:, :, bq:]), axis=-1, keepdims=True)
            p = jnp.exp2(s - m)
            l = jnp.sum(p[:, :, :bq] + p[:, :, bq:], axis=-1, keepdims=True)
            num = jnp.einsum('uqk,ukd->uqd', p.astype(BF16), v, preferred_element_type=F32)
            if not last:
                out = num * (1.0 / l)
                lse = jnp.broadcast_to(m + jnp.log2(l), num.shape)
                for u, base in enumerate(bases):
                    put(out_refs[pi], base, out[u])
                    put(lse_refs[pi], base, lse[u])
                return carry
            m = jnp.broadcast_to(m, num.shape)
            l = jnp.broadcast_to(l, num.shape)
            lses = [stack(ref, bases, bq) for ref in lse_refs]
            top = functools.reduce(jnp.maximum, lses, m)
            w_own = jnp.exp2(m - top)
            numer = w_own * num
            denom = w_own * l
            for ref, lse in zip(out_refs, lses):
                w = jnp.exp2(lse - top)
                numer = numer + w * stack(ref, bases, bq)
                denom = denom + w
            for u, base in enumerate(bases):
                rows_out = pl.ds(pl.multiple_of(base, bq), bq)
                o_ref[rows_out, :] = (numer[u] / denom[u] * ga_ref[rows_out, :]).astype(BF16)
            return carry

        lax.fori_loop(0, t // (bq * ATTN_UNROLL), body, 0, unroll=2)


def _attn_prompt(proj):
    b, _, t, _ = proj.shape
    for w, d in PATTERNS:
        assert w // d == KEYS_PER_PATTERN and t % (2 * KEYS_PER_PATTERN * d) == 0
    assert (t // KEYS_PER_PATTERN) % ATTN_UNROLL == 0
    col = lambda off: pl.BlockSpec((None, None, t, HEAD_DIM),
                                   lambda bi, h, off=off: (bi, off + h, 0, 0))
    bq = KEYS_PER_PATTERN
    qi = jnp.arange(bq, dtype=jnp.int32)[:, None]
    kj = jnp.arange(2 * bq, dtype=jnp.int32)[None, :]
    visible = jnp.stack([kj <= qi, (kj >= qi) & (kj <= qi + bq)])
    bias = jnp.where(visible, 0.0, NEG_INF).astype(F32)
    keep = min(WINDOW_MAX, t)
    tail = pl.BlockSpec((None, keep, N_HEADS, HEAD_DIM), lambda bi, h: (bi, 0, 0, 0),
                        pipeline_mode=pl.Buffered(1))
    tail_shape = jax.ShapeDtypeStruct((b, keep, N_HEADS, HEAD_DIM), F32)
    return pl.pallas_call(
        _attn_prompt_kernel,
        grid=(b, N_HEADS),
        in_specs=[col(0), col(N_HEADS), col(2 * N_HEADS), col(3 * N_HEADS),
                  pl.BlockSpec((2, bq, 2 * bq), lambda bi, h: (0, 0, 0))],
        out_specs=(pl.BlockSpec((None, t, HEAD_DIM), lambda bi, h: (bi, 0, h)), tail, tail),
        out_shape=(jax.ShapeDtypeStruct((b, t, D_ATTN), BF16), tail_shape, tail_shape),
        scratch_shapes=[pltpu.VMEM((t, HEAD_DIM), F32)] * (2 * (len(PATTERNS) - 1)),
        compiler_params=_params("arbitrary", "arbitrary"),
        name="attn_prompt",
    )(proj, proj, proj, proj, bias)


def _pattern_count(delta):
    cnt = jnp.zeros(delta.shape, F32)
    for w, d in PATTERNS:
        assert d & (d - 1) == 0
        ok = (delta >= 0) & (delta <= w) & (jnp.bitwise_and(delta, d - 1) == 0)
        cnt = cnt + ok.astype(F32)
    return cnt


def _sample_key_weights(nh, tq, wb, n_groups, n_dense):
    nq = nh * tq
    r = jnp.arange(nq, dtype=jnp.int32)[:, None]

    def window(groups, group0, residue0):
        c = jnp.arange(groups * SUBLANES * nh, dtype=jnp.int32)[None, :]
        key_head = c % nh
        residue = residue0 + (c // nh) % SUBLANES
        key_row = (group0 + c // (nh * SUBLANES)) * SAMPLE_GROUP + residue
        delta = wb + r % tq - key_row
        return jnp.where(r // tq == key_head, _pattern_count(delta), 0.0)

    c = jnp.arange(nq, dtype=jnp.int32)[None, :]
    new = jnp.where(r // tq == c // tq, _pattern_count(r % tq - c % tq), 0.0)
    return window(n_groups, 0, 0), window(n_dense, n_groups - n_dense, SUBLANES), new


def _attn_sample_kernel(q_ref, kn_ref, vn_ref, ga_ref, ka_ref, kb_ref, va_ref, vb_ref,
                        wa_ref, wb_ref, wn_ref, o_ref):
    nh, tq, _ = q_ref.shape
    nq = nh * tq

    def heads_to_rows(ref):
        return ref[...].reshape(nq, HEAD_DIM)

    def flat(ref):
        g, j, h, d = ref.shape
        return ref[...].reshape(g * j * h, d).astype(BF16)

    parts = (
        (flat(ka_ref), flat(va_ref), wa_ref[...]),
        (flat(kb_ref), flat(vb_ref), wb_ref[...]),
        (heads_to_rows(kn_ref).astype(BF16), heads_to_rows(vn_ref).astype(BF16), wn_ref[...]),
    )
    q = heads_to_rows(q_ref).astype(BF16)
    scores = [jnp.where(wgt > 0, _dot_nt(q, k), NEG_INF) for k, _, wgt in parts]
    m = functools.reduce(jnp.maximum, [jnp.max(s, axis=-1, keepdims=True) for s in scores])
    den = jnp.zeros((nq, 1), F32)
    num = jnp.zeros((nq, HEAD_DIM), F32)
    for s, (_, v, wgt) in zip(scores, parts):
        p = wgt * jnp.exp2(s - m)
        den = den + jnp.sum(p, axis=-1, keepdims=True)
        num = num + _dot(p.astype(BF16), v)
    out = num / den
    for h in range(nh):
        cs = slice(h * HEAD_DIM, (h + 1) * HEAD_DIM)
        o_ref[:, cs] = out[h * tq:(h + 1) * tq] * ga_ref[h]


def _attn_sample(proj, win_k, win_v, li, n_seq, tq):
    wb = win_k.shape[2]
    dense_reach = max(w for w, d in PATTERNS if d < SAMPLE_GROUP)
    assert max(d for _, d in PATTERNS) == SAMPLE_GROUP == 2 * SUBLANES and tq == SUBLANES
    assert wb % SAMPLE_GROUP == 0 and dense_reach % SAMPLE_GROUP == 0 and wb % dense_reach == 0
    n_groups = wb // SAMPLE_GROUP
    n_dense = dense_reach // SAMPLE_GROUP
    win_k = win_k.reshape(-1, n_seq, n_groups, SAMPLE_GROUP, N_HEADS, HEAD_DIM)
    win_v = win_v.reshape(-1, n_seq, n_groups, SAMPLE_GROUP, N_HEADS, HEAD_DIM)
    col = lambda blk: pl.BlockSpec((None, N_HEADS, tq, HEAD_DIM), lambda s, blk=blk: (0, blk, s, 0))
    sparse = pl.BlockSpec((None, None, n_groups, SUBLANES, N_HEADS, HEAD_DIM),
                          lambda s: (li, s, 0, 0, 0, 0))
    dense = pl.BlockSpec((None, None, n_dense, SUBLANES, N_HEADS, HEAD_DIM),
                         lambda s: (li, s, n_groups // n_dense - 1, 1, 0, 0))
    key_weights = _sample_key_weights(N_HEADS, tq, wb, n_groups, n_dense)
    whole = lambda a: pl.BlockSpec(a.shape, lambda s: (0, 0))
    return pl.pallas_call(
        _attn_sample_kernel,
        grid=(n_seq,),
        in_specs=[col(0), col(1), col(2), col(3), sparse, dense, sparse, dense,
                  *[whole(a) for a in key_weights]],
        out_specs=pl.BlockSpec((None, tq, D_ATTN), lambda s: (0, s, 0)),
        out_shape=jax.ShapeDtypeStruct((1, n_seq * tq, D_ATTN), F32),
        compiler_params=_params("arbitrary"),
        name="attn_sample",
    )(proj, proj, proj, proj, win_k, win_k, win_v, win_v, *key_weights)


def _pool_kernel(u_ref, halo_ref, pre_ref, gp_ref, w_ref, s_ref, o_ref, ext_ref, tmp_ref,
                 *, pos0):
    i = pl.program_id(1)
    _, tm, _ = u_ref.shape
    per_group = POOL_GROUP // HEAD_DIM
    pad = SUBLANES
    first = pad + POOL_MAX
    n_ext = first + tm
    assert POOL_MAX // 2 <= pad and n_ext == ext_ref.shape[1]
    lanes = lambda slabs: jnp.concatenate([slabs[s] for s in range(per_group)], axis=-1)

    zeros = jnp.zeros((ext_ref.shape[0], pad, HEAD_DIM), F32)
    ext_ref[:, 0:pad, :] = zeros
    tmp_ref[:, 0:pad, :] = zeros

    @pl.when(i == 0)
    def _():
        for s in range(ext_ref.shape[0]):
            ext_ref[s, pad:first, :] = pre_ref[:, s * HEAD_DIM:(s + 1) * HEAD_DIM]

    @pl.when(i > 0)
    def _():
        ext_ref[:, pad:first, :] = halo_ref[...]

    ext_ref[:, first:, :] = u_ref[...]
    pos = pos0 + i * tm + lax.broadcasted_iota(jnp.int32, (1, tm, 1), 1)

    for gi, w in enumerate(POOL_WINDOWS):
        assert w & (w - 1) == 0
        sl = slice(gi * per_group, (gi + 1) * per_group)
        cols = slice(gi * POOL_GROUP, (gi + 1) * POOL_GROUP)
        u = u_ref[sl]
        src, dst = ext_ref, tmp_ref
        span = 1
        while span < w:
            dst[sl, pad:n_ext, :] = src[sl, pad:n_ext, :] + src[sl, pad - span:n_ext - span, :]
            src, dst = dst, src
            span *= 2
        win = src[sl, first:n_ext, :]
        inv_cnt = 1.0 / jnp.minimum(w, pos + 1).astype(F32)
        pooled = lanes(win * inv_cnt - u)
        mixed = _dot(pooled.astype(BF16), w_ref[gi].astype(BF16)) * s_ref[:, cols]
        o_ref[:, cols] = (mixed * lanes(gp_ref[sl])).astype(o_ref.dtype)


def _pool(proj, prefix, pool_w, pool_scale, li, seq_rows, tm, pos0, out_dtype):
    g_, _, r_, _ = proj.shape
    n_seq = g_ * (r_ // seq_rows)
    nt = seq_rows // tm
    per_g = r_ // seq_rows
    ng = len(POOL_WINDOWS)
    n_slab = D_POOL // HEAD_DIM
    u_blk, gp_blk = 4 * D_ATTN // D_POOL, 4 * D_ATTN // D_POOL + 1
    hb = tm // POOL_MAX

    def slab_map(blk):
        return lambda s, i: (s // per_g, blk, (s % per_g) * nt + i, 0)

    def halo_map(s, i):
        return (s // per_g, u_blk, jnp.maximum(((s % per_g) * nt + i) * hb - 1, 0), 0)

    return pl.pallas_call(
        functools.partial(_pool_kernel, pos0=pos0),
        grid=(n_seq, nt),
        in_specs=[
            pl.BlockSpec((None, n_slab, tm, HEAD_DIM), slab_map(u_blk)),
            pl.BlockSpec((None, n_slab, POOL_MAX, HEAD_DIM), halo_map),
            pl.BlockSpec((None, POOL_MAX, D_POOL), lambda s, i: (s, 0, 0)),
            pl.BlockSpec((None, n_slab, tm, HEAD_DIM), slab_map(gp_blk)),
            pl.BlockSpec((None, ng, POOL_GROUP, POOL_GROUP), lambda s, i: (li, 0, 0, 0)),
            pl.BlockSpec((None, 1, D_POOL), lambda s, i: (li, 0, 0)),
        ],
        out_specs=pl.BlockSpec(
            (None, tm, D_POOL), lambda s, i: (s // per_g, (s % per_g) * nt + i, 0)),
        out_shape=jax.ShapeDtypeStruct((g_, r_, D_POOL), out_dtype),
        scratch_shapes=[pltpu.VMEM((n_slab, SUBLANES + POOL_MAX + tm, HEAD_DIM), F32)] * 2,
        compiler_params=_params("arbitrary", "arbitrary"),
        name="pool_mixer",
    )(proj, proj, prefix, proj, pool_w, pool_scale)


def _trunk(x, mods, mod_row0, pos0, state, weights, matmul_weights, bb, tm, tm_out):
    norm_g, pool_w, pool_scale, conv_w, final_g = weights
    bk, t, _ = x.shape
    fresh = state is None
    assert mod_row0 % bb == 0
    cos, sin = _rope_tables(t, pos0)
    if bb > 1:
        cos, sin = jnp.tile(cos, (bb, 1)), jnp.tile(sin, (bb, 1))
    coef = _rotation_coefficients(cos, sin)
    new_k, new_v, new_pool, new_conv = [], [], [], []
    made = {}
    for layer in range(DEPTH):
        li = layer // 2
        where = (layer, li, mod_row0 // bb)
        if layer % 2 == 0:
            proj, made["in_even", li] = _in_even(
                x, mods, norm_g, matmul_weights["in_even", li], coef, where, bb, tm)

            def seq_rows(first_slab, n_last, proj=proj):
                g = proj.shape[0]
                slabs = slice(first_slab, first_slab + N_HEADS)
                if g == bk:
                    return jnp.transpose(proj[:, slabs, t - n_last:], (0, 2, 1, 3))
                s = proj[:, slabs].reshape(g, N_HEADS, bk // g, t, HEAD_DIM)[:, :, :, t - n_last:]
                return jnp.transpose(s, (0, 2, 3, 1, 4)).reshape(bk, n_last, N_HEADS, HEAD_DIM)

            if fresh:
                pool_prefix = jnp.zeros((bk, POOL_MAX - 1, D_POOL), F32)
                attn, k, v = _attn_prompt(proj)
                u_tail = seq_rows(4 * N_HEADS, POOL_MAX - 1).reshape(bk, -1, D_POOL)
            else:
                win_k, win_v, pool_st, _ = state
                pool_prefix = pool_st[li]
                attn = _attn_sample(proj, win_k, win_v, li, bk, t)
                k = seq_rows(N_HEADS, t)
                v = seq_rows(2 * N_HEADS, t)
                u_tail = jnp.concatenate(
                    [pool_prefix, seq_rows(4 * N_HEADS, t).reshape(bk, t, D_POOL)],
                    axis=1)[:, -(POOL_MAX - 1):]
            new_pool.append(u_tail)
            new_k.append(k)
            new_v.append(v)
            prefix = jnp.pad(pool_prefix, ((0, 0), (1, 0), (0, 0)))
            pooled = _pool(proj, prefix, pool_w, pool_scale, li, t,
                           min(t, 1024), pos0, BF16 if fresh else F32)
            x, made["out_even", li] = _out_proj(
                attn, 0, pooled, 0, matmul_weights["out_even", li], x, mods, final_g, where,
                bb, tm_out)
        else:
            if fresh:
                prefix = jnp.zeros((bk, SUBLANES, D_CONV), F32)
            else:
                prefix = jnp.pad(state[3][li], ((0, 0), (SUBLANES - (CONV_WIDTH - 1), 0), (0, 0)))
            y, tail, made["in_odd", li] = _in_odd(
                x, mods, norm_g, matmul_weights["in_odd", li], conv_w, prefix, where, bb, tm)
            new_conv.append(tail[:, SUBLANES - (CONV_WIDTH - 1):])
            x, made["out_odd", li] = _out_proj(
                y, 0, y, 1, matmul_weights["out_odd", li], x, mods, final_g, where, bb, tm_out)
    outs = (x, jnp.stack(new_k), jnp.stack(new_v), jnp.stack(new_pool), jnp.stack(new_conv))
    return outs, made


def kernel(x_prompt, x_sample, c_prompt, c_sample, state_win_k, state_win_v, state_pool,
           state_conv, ada_w, ada_b, norm_g, w_in_even, w_out_even, pool_w, pool_scale,
           w_in_odd, conv_w, w_out_odd, final_g):
    nb, t_p, _ = x_prompt.shape
    ns, t_s, _ = x_sample.shape
    n_c = nb + ns
    c_rows = -(-n_c // SUBLANES) * SUBLANES
    c_all = jnp.pad(jnp.concatenate([c_sample, c_prompt], axis=0), ((0, c_rows - n_c), (0, 0)))
    mods = _ada_modulation(c_all, ada_w, ada_b)[:, :, None, :]
    weights = (norm_g[:, None, :], pool_w, pool_scale[:, None, :], conv_w, final_g[None, :])
    stacks = {"in_even": w_in_even, "out_even": w_out_even, "in_odd": w_in_odd,
              "out_odd": w_out_odd}
    f32_weights = {(kind, li): w for kind, w in stacks.items() for li in range(w.shape[0])}

    state = (state_win_k, state_win_v, state_pool, state_conv)
    (y_s, ksm, vsm, pools, convs), bf16_weights = _trunk(
        x_sample, mods, 0, PAST_LEN, state, weights, f32_weights, ns, t_s, t_s)
    (y_p, kp, vp, poolp, convp), _ = _trunk(
        x_prompt, mods, ns, 0, None, weights, bf16_weights, 1, 1024, 512)
    return (y_p, y_s, kp, vp, poolp, convp, ksm, vsm, pools, convs)
```
